```python
import jax, jax.numpy as jnp
from jax import lax
import numpy as np

D_MODEL = 1024
BATCH = 8
SEQ = 4096
DEPTH = 2

CTX_LEN = 256
GRID_W = 64
D_FF = 2816
N_MOD = 9
EPS = 1e-6
ROPE_BASE = 10000.0
Q_BLOCK = 128
MLA_HEADS = 8
MLA_NOPE = 64
MLA_ROPE = 32
MLA_V = 64
MLA_Q_LORA = 384
MLA_KV_LORA = 256
CONV_CH = 512
CONV_WIDTH = 31
HEAD_DIM = 64
GQA_HEADS = 8
GQA_KV_HEADS = 2
NA_HEADS = 8
WIN_H = 8
WIN_W = 16

EVEN_KV_COLS = MLA_KV_LORA + MLA_ROPE
EVEN_CONV_OFF = EVEN_KV_COLS + MLA_Q_LORA
EVEN_COLS = EVEN_CONV_OFF + 2 * CONV_CH
GQA_KV_W = GQA_KV_HEADS * HEAD_DIM
GQA_Q_W = GQA_HEADS * HEAD_DIM
NA_W = NA_HEADS * HEAD_DIM
ODD_KV_COLS = 2 * GQA_KV_W + 2 * NA_W
ODD_COLS = ODD_KV_COLS + GQA_Q_W + NA_W
MIX_OUT = MLA_HEADS * MLA_V + CONV_CH

kernel_name = "hybrid_conv_mla_gqa_natten_dit_block"


def rms_norm(x, g):
    xf = x.astype(jnp.float32)
    y = xf * lax.rsqrt(jnp.mean(xf * xf, axis=-1, keepdims=True) + EPS)
    return (y * g.astype(jnp.float32)).astype(x.dtype)


def layer_norm(x, g, b):
    xf = x.astype(jnp.float32)
    mu = jnp.mean(xf, axis=-1, keepdims=True)
    var = jnp.mean(jnp.square(xf - mu), axis=-1, keepdims=True)
    y = (xf - mu) * lax.rsqrt(var + EPS)
    return (y * g.astype(jnp.float32) + b.astype(jnp.float32)).astype(x.dtype)


def axial_rope_table(n_tokens, dim):
    t = jnp.arange(n_tokens)
    row = (t // GRID_W).astype(jnp.float32)
    col = (t % GRID_W).astype(jnp.float32)
    n_pairs = dim // 4
    inv = ROPE_BASE ** (-jnp.arange(n_pairs, dtype=jnp.float32) / n_pairs)
    ang = jnp.concatenate([row[:, None] * inv, col[:, None] * inv], axis=-1)
    return jnp.cos(ang), jnp.sin(ang)


def apply_rope(x, cos, sin):
    xp = x.reshape(*x.shape[:-1], x.shape[-1] // 2, 2)
    x1, x2 = xp[..., 0], xp[..., 1]
    cs = cos[None, :, None, :].astype(x.dtype)
    sn = sin[None, :, None, :].astype(x.dtype)
    return jnp.stack([x1 * cs - x2 * sn, x1 * sn + x2 * cs], axis=-1).reshape(x.shape)


def _attend(q, k, v):
    scale = q.shape[-1] ** -0.5
    s = jnp.einsum('bqhgd,bkhd->bhgqk', q, k).astype(jnp.float32) * scale
    p = jax.nn.softmax(s, axis=-1).astype(v.dtype)
    return jnp.einsum('bhgqk,bkhd->bqhgd', p, v)


def context_attention(q, k, v):
    B, L, Hk, G, _ = q.shape
    return _attend(q, k, v).reshape(B, L, Hk * G, v.shape[-1])


def latent_attention(q, k, v, k_ctx, v_ctx):
    B, S, Hk, G, d = q.shape
    k_all = jnp.concatenate([k_ctx, k], axis=1)
    v_all = jnp.concatenate([v_ctx, v], axis=1)
    qb = q.reshape(B, S // Q_BLOCK, Q_BLOCK, Hk, G, d).transpose(1, 0, 2, 3, 4, 5)
    o = lax.map(lambda qi: _attend(qi, k_all, v_all), qb)
    return o.transpose(1, 0, 2, 3, 4, 5).reshape(B, S, Hk * G, v.shape[-1])


def neighbourhood_attention(q, k, v, k_ctx, v_ctx, rpb):
    B, S, H, d = q.shape
    rows = S // GRID_W
    kh = min(WIN_H, rows)
    scale = d ** -0.5
    qg = q.reshape(B, rows, GRID_W, H, d).transpose(1, 0, 2, 3, 4)
    kg = k.reshape(B, rows, GRID_W, H, d)
    vg = v.reshape(B, rows, GRID_W, H, d)
    col = jnp.arange(GRID_W)
    cs = jnp.clip(col - WIN_W // 2, 0, GRID_W - WIN_W)
    col_idx = cs[:, None] + jnp.arange(WIN_W)[None, :]
    dc = col_idx - col[:, None] + (WIN_W - 1)
    n_ctx = k_ctx.shape[1]

    def row_block(args):
        r, q_r = args
        rs = jnp.clip(r - kh // 2, 0, rows - kh)
        k_rows = lax.dynamic_slice_in_dim(kg, rs, kh, axis=1)
        v_rows = lax.dynamic_slice_in_dim(vg, rs, kh, axis=1)
        k_nb = k_rows[:, :, col_idx]
        v_nb = v_rows[:, :, col_idx]
        s_nb = jnp.einsum('bqhd,bjqwhd->bhqjw', q_r, k_nb).astype(jnp.float32) * scale
        dr = rs + jnp.arange(kh) - r + (WIN_H - 1)
        bias = rpb[:, dr[None, :, None], dc[:, None, :]]
        s_nb = (s_nb + bias[None].astype(jnp.float32)).reshape(B, H, GRID_W, kh * WIN_W)
        s_ctx = jnp.einsum('bqhd,bkhd->bhqk', q_r, k_ctx).astype(jnp.float32) * scale
        p = jax.nn.softmax(jnp.concatenate([s_ctx, s_nb], axis=-1), axis=-1).astype(v.dtype)
        p_ctx = p[..., :n_ctx]
        p_nb = p[..., n_ctx:].reshape(B, H, GRID_W, kh, WIN_W)
        return (jnp.einsum('bhqk,bkhd->bqhd', p_ctx, v_ctx)
                + jnp.einsum('bhqjw,bjqwhd->bqhd', p_nb, v_nb))

    o = lax.map(row_block, (jnp.arange(rows), qg))
    return o.transpose(1, 0, 2, 3, 4).reshape(B, S, H, d)


def swiglu(h, w_in, w_out):
    g, u = jnp.split(h @ w_in, 2, axis=-1)
    return (jax.nn.silu(g) * u) @ w_out


def conv_module(u, prm):
    u = u + prm["conv_glu_b"]
    a, g = jnp.split(u, 2, axis=-1)
    y = a * jax.nn.sigmoid(g)
    y = lax.conv_general_dilated(
        y, prm["conv_dw_w"][:, None, :].astype(y.dtype), window_strides=(1,),
        padding=[(CONV_WIDTH // 2, CONV_WIDTH // 2)],
        dimension_numbers=("NWC", "WIO", "NWC"), feature_group_count=CONV_CH) + prm["conv_dw_b"]
    y = layer_norm(y, prm["conv_ln_g"], prm["conv_ln_b"])
    return jax.nn.silu(y)


def mla_kv(pp, prm, rope):
    B, L, _ = pp.shape
    ckv = rms_norm(pp[..., :MLA_KV_LORA], prm["mla_kv_norm"])
    kvu = (ckv @ prm["mla_w_ukv"]).reshape(B, L, MLA_HEADS, MLA_NOPE + MLA_V)
    k_nope = rms_norm(kvu[..., :MLA_NOPE], prm["mla_k_gain"][:MLA_NOPE])
    v = kvu[..., MLA_NOPE:]
    k_rope = rms_norm(pp[..., MLA_KV_LORA:EVEN_KV_COLS], prm["mla_k_gain"][MLA_NOPE:])[:, :, None, :]
    if rope is not None:
        k_rope = apply_rope(k_rope, *rope)
    k = jnp.concatenate([k_nope, jnp.broadcast_to(k_rope, (B, L, MLA_HEADS, MLA_ROPE))], axis=-1)
    return k, v


def mla_q(pp, prm, rope):
    B, L, _ = pp.shape
    cq = rms_norm(pp[..., EVEN_KV_COLS:EVEN_CONV_OFF], prm["mla_q_norm"])
    q = (cq @ prm["mla_w_uq"]).reshape(B, L, MLA_HEADS, MLA_NOPE + MLA_ROPE)
    q_nope = rms_norm(q[..., :MLA_NOPE], prm["mla_q_gain"][:MLA_NOPE])
    q_rope = rms_norm(q[..., MLA_NOPE:], prm["mla_q_gain"][MLA_NOPE:])
    if rope is not None:
        q_rope = apply_rope(q_rope, *rope)
    return jnp.concatenate([q_nope, q_rope], axis=-1)[:, :, :, None, :]


def even_mixer(hl, hc, prm, ctx_out, ropes):
    rope_mla, _ = ropes
    B, S, _ = hl.shape
    pl = hl @ prm["w_in"]
    pc = hc @ (prm["w_in"] if ctx_out else prm["w_in"][:, :EVEN_KV_COLS])
    k_l, v_l = mla_kv(pl, prm, rope_mla)
    k_c, v_c = mla_kv(pc, prm, None)
    att_l = latent_attention(mla_q(pl, prm, rope_mla), k_l, v_l, k_c, v_c).reshape(B, S, -1)
    conv_l = conv_module(pl[..., EVEN_CONV_OFF:], prm)
    ol = jnp.concatenate([att_l, conv_l], axis=-1) @ prm["w_out"]
    oc = None
    if ctx_out:
        Bc, Lc, _ = hc.shape
        att_c = context_attention(mla_q(pc, prm, None), k_c, v_c).reshape(Bc, Lc, -1)
        conv_c = conv_module(pc[..., EVEN_CONV_OFF:], prm)
        oc = jnp.concatenate([att_c, conv_c], axis=-1) @ prm["w_out"]
    return ol, oc


def odd_kv(pp, prm, rope):
    B, L, _ = pp.shape
    o1, o2, o3 = GQA_KV_W, 2 * GQA_KV_W, 2 * GQA_KV_W + NA_W
    ck = rms_norm(pp[..., :o1].reshape(B, L, GQA_KV_HEADS, HEAD_DIM), prm["gqa_k_gain"])
    if rope is not None:
        ck = apply_rope(ck, *rope)
    cv = pp[..., o1:o2].reshape(B, L, GQA_KV_HEADS, HEAD_DIM)
    nk = rms_norm(pp[..., o2:o3].reshape(B, L, NA_HEADS, HEAD_DIM), prm["na_k_gain"])
    nv = pp[..., o3:ODD_KV_COLS].reshape(B, L, NA_HEADS, HEAD_DIM)
    return ck, cv, nk, nv


def odd_q(pp, prm, rope):
    B, L, _ = pp.shape
    cq = rms_norm(pp[..., ODD_KV_COLS:ODD_KV_COLS + GQA_Q_W].reshape(B, L, GQA_HEADS, HEAD_DIM), prm["gqa_q_gain"])
    if rope is not None:
        cq = apply_rope(cq, *rope)
    cq = cq.reshape(B, L, GQA_KV_HEADS, GQA_HEADS // GQA_KV_HEADS, HEAD_DIM)
    nq = rms_norm(pp[..., ODD_KV_COLS + GQA_Q_W:].reshape(B, L, NA_HEADS, HEAD_DIM), prm["na_q_gain"])
    return cq, nq


def odd_mixer(hl, hc, prm, ctx_out, ropes):
    _, rope_hd = ropes
    B, S, _ = hl.shape
    pl = hl @ prm["w_in"]
    pc = hc @ (prm["w_in"] if ctx_out else prm["w_in"][:, :ODD_KV_COLS])
    ck_l, cv_l, nk_l, nv_l = odd_kv(pl, prm, rope_hd)
    ck_c, cv_c, nk_c, nv_c = odd_kv(pc, prm, None)
    cq_l, nq_l = odd_q(pl, prm, rope_hd)
    gqa_l = latent_attention(cq_l, ck_l, cv_l, ck_c, cv_c).reshape(B, S, -1)
    na_l = neighbourhood_attention(nq_l, nk_l, nv_l, nk_c, nv_c, prm["na_rpb"]).reshape(B, S, -1)
    ol = jnp.concatenate([gqa_l, na_l], axis=-1) @ prm["w_out"]
    oc = None
    if ctx_out:
        Bc, Lc, _ = hc.shape
        cq_c, nq_c = odd_q(pc, prm, None)
        gqa_c = context_attention(cq_c, ck_c, cv_c).reshape(Bc, Lc, -1)
        na_c = context_attention(nq_c[:, :, :, None, :], nk_c, nv_c).reshape(Bc, Lc, -1)
        oc = jnp.concatenate([gqa_c, na_c], axis=-1) @ prm["w_out"]
    return ol, oc


def adaln(cond, prm):
    mod = jax.nn.silu(cond) @ prm["mod_w"] + prm["mod_b"]
    return jnp.split(mod, N_MOD, axis=-1)


def modulate(h, shift, scale):
    return h * (1.0 + scale) + shift


def trunk_layer(xl, xc, c, c_ctx, prm, even, ctx_out, ropes):
    mod_l = adaln(c[:, None, :], prm)
    mod_c = adaln(c_ctx[None, None, :], prm)

    def half_ffn(x, m, name):
        sh, sc, g = m
        h = modulate(rms_norm(x, prm[name + "_norm"]), sh, sc)
        return x + 0.5 * g * swiglu(h, prm[name + "_w_in"], prm[name + "_w_out"])

    xl = half_ffn(xl, mod_l[0:3], "ffn1")
    xc = half_ffn(xc, mod_c[0:3], "ffn1")
    hl = modulate(rms_norm(xl, prm["mix_norm"]), mod_l[3], mod_l[4])
    hc = modulate(rms_norm(xc, prm["mix_norm"]), mod_c[3], mod_c[4])
    mixer = even_mixer if even else odd_mixer
    ol, oc = mixer(hl, hc, prm, ctx_out, ropes)
    xl = xl + mod_l[5] * ol
    xl = half_ffn(xl, mod_l[6:9], "ffn2")
    if ctx_out:
        xc = xc + mod_c[5] * oc
        xc = half_ffn(xc, mod_c[6:9], "ffn2")
    return xl, xc


def _layer_params(key, i):
    ks = iter(jax.random.split(key, 32))

    def nrm(shape, scale):
        return jax.random.normal(next(ks), shape, jnp.float32) * scale

    def gain(n):
        return 1.0 + 0.02 * jax.random.normal(next(ks), (n,), jnp.float32)

    D = D_MODEL
    pre = f"l{i}_"
    p = {}
    p[pre + "mod_w"] = nrm((D, N_MOD * D), 0.5 * D ** -0.5)
    p[pre + "mod_b"] = nrm((N_MOD * D,), 0.02)
    p[pre + "ffn1_norm"] = gain(D)
    p[pre + "ffn1_w_in"] = nrm((D, 2 * D_FF), D ** -0.5)
    p[pre + "ffn1_w_out"] = nrm((D_FF, D), D_FF ** -0.5)
    p[pre + "mix_norm"] = gain(D)
    if i % 2 == 0:
        p[pre + "w_in"] = nrm((D, EVEN_COLS), D ** -0.5)
        p[pre + "mla_q_norm"] = gain(MLA_Q_LORA)
        p[pre + "mla_w_uq"] = nrm((MLA_Q_LORA, MLA_HEADS * (MLA_NOPE + MLA_ROPE)), MLA_Q_LORA ** -0.5)
        p[pre + "mla_kv_norm"] = gain(MLA_KV_LORA)
        p[pre + "mla_w_ukv"] = nrm((MLA_KV_LORA, MLA_HEADS * (MLA_NOPE + MLA_V)), MLA_KV_LORA ** -0.5)
        p[pre + "mla_q_gain"] = gain(MLA_NOPE + MLA_ROPE)
        p[pre + "mla_k_gain"] = gain(MLA_NOPE + MLA_ROPE)
        p[pre + "conv_glu_b"] = nrm((2 * CONV_CH,), 0.02)
        p[pre + "conv_dw_w"] = nrm((CONV_WIDTH, CONV_CH), CONV_WIDTH ** -0.5)
        p[pre + "conv_dw_b"] = nrm((CONV_CH,), 0.02)
        p[pre + "conv_ln_g"] = gain(CONV_CH)
        p[pre + "conv_ln_b"] = nrm((CONV_CH,), 0.02)
    else:
        p[pre + "w_in"] = nrm((D, ODD_COLS), D ** -0.5)
        p[pre + "gqa_q_gain"] = gain(HEAD_DIM)
        p[pre + "gqa_k_gain"] = gain(HEAD_DIM)
        p[pre + "na_q_gain"] = gain(HEAD_DIM)
        p[pre + "na_k_gain"] = gain(HEAD_DIM)
        p[pre + "na_rpb"] = nrm((NA_HEADS, 2 * WIN_H - 1, 2 * WIN_W - 1), 0.1)
    p[pre + "w_out"] = nrm((MIX_OUT, D), MIX_OUT ** -0.5)
    p[pre + "ffn2_norm"] = gain(D)
    p[pre + "ffn2_w_in"] = nrm((D, 2 * D_FF), D ** -0.5)
    p[pre + "ffn2_w_out"] = nrm((D_FF, D), D_FF ** -0.5)
    return p


def setup_inputs(seed: int = 0) -> dict:
    key = jax.random.key(seed)
    k_x, k_c, k_ctx, k_cc, k_layers = jax.random.split(key, 5)
    inputs = {
        "x": jax.random.normal(k_x, (BATCH, SEQ, D_MODEL), jnp.float32),
        "c": jax.random.normal(k_c, (BATCH, D_MODEL), jnp.float32),
        "ctx": jax.random.normal(k_ctx, (BATCH, CTX_LEN, D_MODEL), jnp.float32),
        "c_ctx": jax.random.normal(k_cc, (D_MODEL,), jnp.float32),
    }
    layer_keys = jax.random.split(k_layers, DEPTH)
    for i in range(DEPTH):
        inputs.update(_layer_params(layer_keys[i], i))
    return inputs


def reference(x, c, ctx, c_ctx,
              l0_mod_w, l0_mod_b, l0_ffn1_norm, l0_ffn1_w_in, l0_ffn1_w_out, l0_mix_norm, l0_w_in,
              l0_mla_q_norm, l0_mla_w_uq, l0_mla_kv_norm, l0_mla_w_ukv, l0_mla_q_gain, l0_mla_k_gain,
              l0_conv_glu_b, l0_conv_dw_w, l0_conv_dw_b, l0_conv_ln_g, l0_conv_ln_b,
              l0_w_out, l0_ffn2_norm, l0_ffn2_w_in, l0_ffn2_w_out,
              l1_mod_w, l1_mod_b, l1_ffn1_norm, l1_ffn1_w_in, l1_ffn1_w_out, l1_mix_norm, l1_w_in,
              l1_gqa_q_gain, l1_gqa_k_gain, l1_na_q_gain, l1_na_k_gain, l1_na_rpb,
              l1_w_out, l1_ffn2_norm, l1_ffn2_w_in, l1_ffn2_w_out):
    layers = (
        dict(mod_w=l0_mod_w, mod_b=l0_mod_b, ffn1_norm=l0_ffn1_norm, ffn1_w_in=l0_ffn1_w_in,
             ffn1_w_out=l0_ffn1_w_out, mix_norm=l0_mix_norm, w_in=l0_w_in,
             mla_q_norm=l0_mla_q_norm, mla_w_uq=l0_mla_w_uq, mla_kv_norm=l0_mla_kv_norm,
             mla_w_ukv=l0_mla_w_ukv, mla_q_gain=l0_mla_q_gain, mla_k_gain=l0_mla_k_gain,
             conv_glu_b=l0_conv_glu_b, conv_dw_w=l0_conv_dw_w, conv_dw_b=l0_conv_dw_b,
             conv_ln_g=l0_conv_ln_g, conv_ln_b=l0_conv_ln_b, w_out=l0_w_out,
             ffn2_norm=l0_ffn2_norm, ffn2_w_in=l0_ffn2_w_in, ffn2_w_out=l0_ffn2_w_out),
        dict(mod_w=l1_mod_w, mod_b=l1_mod_b, ffn1_norm=l1_ffn1_norm, ffn1_w_in=l1_ffn1_w_in,
             ffn1_w_out=l1_ffn1_w_out, mix_norm=l1_mix_norm, w_in=l1_w_in,
             gqa_q_gain=l1_gqa_q_gain, gqa_k_gain=l1_gqa_k_gain, na_q_gain=l1_na_q_gain,
             na_k_gain=l1_na_k_gain, na_rpb=l1_na_rpb, w_out=l1_w_out,
             ffn2_norm=l1_ffn2_norm, ffn2_w_in=l1_ffn2_w_in, ffn2_w_out=l1_ffn2_w_out),
    )
    S = x.shape[1]
    ropes = (axial_rope_table(S, MLA_ROPE), axial_rope_table(S, HEAD_DIM))
    xl, xc = x, ctx
    for i in range(DEPTH):
        xl, xc = trunk_layer(xl, xc, c, c_ctx, layers[i], even=(i % 2 == 0),
                             ctx_out=(i < DEPTH - 1), ropes=ropes)
    return xl
```

```python
import functools
import math

import numpy as np
import jax
import jax.numpy as jnp
from jax import lax
from jax.experimental import pallas as pl
from jax.experimental.pallas import tpu as pltpu

F32 = jnp.float32
BF16 = jnp.bfloat16

D_MODEL = 1024
CTX_LEN = 256
GRID_W = 64
D_FF = 2816
N_MOD = 9
EPS = 1e-6
ROPE_BASE = 10000.0
MLA_HEADS = 8
MLA_NOPE = 64
MLA_ROPE = 32
MLA_V = 64
MLA_Q_LORA = 384
MLA_KV_LORA = 256
CONV_CH = 512
CONV_WIDTH = 31
HEAD_DIM = 64
GQA_HEADS = 8
GQA_KV_HEADS = 2
NA_HEADS = 8
WIN_H = 8
WIN_W = 16

LANES = 128
MXU_DIM = 256
VMEM_LIMIT = 56 * 1024 * 1024
LOG2E = math.log2(math.e)
NEG_BIG = -1e30

FFN_CHUNK = 256
TOKEN_TILE = 512
ATT_TQ = 512
ATT_TK = 512
NA_ROWS = 4
NA_WIN_ROWS = 12
CONV_TILE = 64
CONV_HALO = 16


def _params(*sem):
    return pltpu.CompilerParams(dimension_semantics=sem, vmem_limit_bytes=VMEM_LIMIT)


def _resident(shape):
    return pl.BlockSpec(shape, lambda *_: (0,) * len(shape), pipeline_mode=pl.Buffered(1))


def _silu(x):
    return x * jax.nn.sigmoid(x)


def _rms_rows(x):
    return x * lax.rsqrt(jnp.mean(x * x, axis=-1, keepdims=True) + EPS)


def _seg_sumsq(x, g):
    outs = []
    for p in range(x.shape[-1] // MXU_DIM):
        sq = x[:, p * MXU_DIM:(p + 1) * MXU_DIM]
        sq = sq * sq
        hi = sq.astype(BF16)
        lo = (sq - hi.astype(F32)).astype(BF16)
        outs.append(jnp.dot(hi, g, preferred_element_type=F32) + jnp.dot(lo, g, preferred_element_type=F32))
    return outs[0] if len(outs) == 1 else jnp.concatenate(outs, axis=-1)


def _swap_halves(x, half, lo):
    lane = lax.broadcasted_iota(jnp.int32, x.shape, 1)
    first = ((lane - lo) & (2 * half - 1)) < half
    return jnp.where(first, pltpu.roll(x, LANES - half, 1), pltpu.roll(x, half, 1))


def _mod_kernel(c_ref, w_ref, b_ref, o_ref):
    a = _silu(c_ref[...]).astype(BF16)
    o_ref[...] = jnp.dot(a, w_ref[...].astype(BF16), preferred_element_type=F32) + b_ref[...]


def _modulation(cond, mod_w, mod_b):
    rows, d = cond.shape
    n = mod_w.shape[1]
    return pl.pallas_call(
        _mod_kernel,
        grid=(n // d,),
        in_specs=[pl.BlockSpec((rows, d), lambda j: (0, 0)),
                  pl.BlockSpec((d, d), lambda j: (0, j)),
                  pl.BlockSpec((1, d), lambda j: (0, j))],
        out_specs=pl.BlockSpec((rows, d), lambda j: (0, j)),
        out_shape=jax.ShapeDtypeStruct((rows, n), F32),
        compiler_params=_params("arbitrary"),
    )(cond, mod_w, mod_b.reshape(1, n))


def _ffn_kernel(x_ref, mod_ref, g_ref, win_ref, wout_ref, o_ref, h_ref, a_ref, *, n_chunks):
    x = x_ref[0]
    mod = mod_ref[0]
    h = _rms_rows(x) * g_ref[...]
    h_ref[...] = (h * (1.0 + mod[1:2]) + mod[0:1]).astype(BF16)
    for c in range(n_chunks):
        gate = jnp.dot(h_ref[...], win_ref[c], preferred_element_type=F32)
        up = jnp.dot(h_ref[...], win_ref[n_chunks + c], preferred_element_type=F32)
        a_ref[:, c * FFN_CHUNK:(c + 1) * FFN_CHUNK] = (_silu(gate) * up).astype(BF16)
    y = jnp.dot(a_ref[...], wout_ref[...], preferred_element_type=F32)
    o_ref[0] = x + (0.5 * mod[2:3]) * y


def _half_ffn(x, mod, norm_g, w_in, w_out):
    b, l, d = x.shape
    tm = min(TOKEN_TILE, l)
    n_chunks = D_FF // FFN_CHUNK
    win = w_in.astype(BF16).reshape(d, 2 * n_chunks, FFN_CHUNK).transpose(1, 0, 2)
    wout = w_out.astype(BF16)
    return pl.pallas_call(
        functools.partial(_ffn_kernel, n_chunks=n_chunks),
        grid=(b, l // tm),
        in_specs=[pl.BlockSpec((1, tm, d), lambda i, j: (i, j, 0)),
                  pl.BlockSpec((1, 3, d), lambda i, j: (i, 0, 0)),
                  _resident((1, d)),
                  _resident((2 * n_chunks, d, FFN_CHUNK)),
                  _resident((D_FF, d))],
        out_specs=pl.BlockSpec((1, tm, d), lambda i, j: (i, j, 0)),
        out_shape=jax.ShapeDtypeStruct((b, l, d), F32),
        scratch_shapes=[pltpu.VMEM((tm, d), BF16), pltpu.VMEM((tm, D_FF), BF16)],
        compiler_params=_params("parallel", "parallel"),
    )(x, mod, norm_g.reshape(1, d), win, wout)


E_KR = MLA_KV_LORA
E_CQ = E_KR + LANES
E_CA = E_CQ + MLA_Q_LORA
E_CG = E_CA + CONV_CH
E_COLS = E_CG + CONV_CH
HEADS_W = MLA_HEADS * LANES


def _even_proj_kernel(x_ref, mod_ref, vec_ref, win_ref, wuk_ref, wuv_ref, wuq_ref, g_ref, c_ref, s_ref,
                      q_ref, k_ref, v_ref, y_ref):
    x = x_ref[0]
    mod = mod_ref[0]
    h = _rms_rows(x) * vec_ref[0:1, :]
    h = (h * (1.0 + mod[1:2]) + mod[0:1]).astype(BF16)
    pp = jnp.dot(h, win_ref[...], preferred_element_type=F32)
    cos = c_ref[...]
    sin = s_ref[...]
    g = g_ref[...]

    ckv = (_rms_rows(pp[:, 0:E_KR]) * vec_ref[1:2, 0:MLA_KV_LORA]).astype(BF16)
    kn = jnp.dot(ckv, wuk_ref[...], preferred_element_type=F32)
    vv = jnp.dot(ckv, wuv_ref[...], preferred_element_type=F32)
    kn = kn * lax.rsqrt(_seg_sumsq(kn, g) * vec_ref[6:7, :] + EPS) * vec_ref[3:4, :]
    kr = pp[:, E_KR:E_CQ]
    kr = kr * lax.rsqrt(jnp.sum(kr * kr, axis=-1, keepdims=True) * (1.0 / MLA_ROPE) + EPS) * vec_ref[4:5, 0:LANES]
    kr = kr * cos + _swap_halves(kr, MLA_ROPE // 2, MLA_NOPE) * sin
    v_ref[0] = (vv + vec_ref[8:9, :]).astype(BF16)

    cq = (_rms_rows(pp[:, E_CQ:E_CA]) * vec_ref[2:3, 0:MLA_Q_LORA]).astype(BF16)
    q = jnp.dot(cq, wuq_ref[...], preferred_element_type=F32)
    q = q * lax.rsqrt(_seg_sumsq(q, g) * vec_ref[6:7, :] + EPS) * vec_ref[5:6, :]
    for hd in range(MLA_HEADS):
        sl = slice(hd * LANES, (hd + 1) * LANES)
        k_ref[0, :, sl] = (kn[:, sl] + kr).astype(BF16)
        qh = q[:, sl]
        qh = qh * cos + _swap_halves(qh, MLA_ROPE // 2, MLA_NOPE) * sin
        q_ref[0, :, sl] = qh.astype(BF16)

    glu_b = vec_ref[7:8, :]
    y_ref[0] = (pp[:, E_CA:E_CG] + glu_b[:, 0:CONV_CH]) * jax.nn.sigmoid(pp[:, E_CG:E_COLS] + glu_b[:, CONV_CH:])


def _even_weights(prm):
    d = D_MODEL
    ev = np.arange(0, MLA_ROPE, 2)
    od = np.arange(1, MLA_ROPE, 2)
    zero_col = prm["w_in"].shape[1]
    cols = np.concatenate([
        np.arange(MLA_KV_LORA),
        np.full(MLA_NOPE, zero_col), MLA_KV_LORA + ev, MLA_KV_LORA + od, np.full(LANES - MLA_NOPE - MLA_ROPE, zero_col),
        np.arange(MLA_KV_LORA + MLA_ROPE, zero_col)])
    w_ext = jnp.concatenate([prm["w_in"], jnp.zeros((d, 1), F32)], axis=1)
    win = jnp.take(w_ext, cols, axis=1).astype(BF16)

    qd = MLA_NOPE + MLA_ROPE
    zq = MLA_HEADS * qd
    qcols = np.concatenate([np.concatenate([h * qd + np.arange(MLA_NOPE), h * qd + MLA_NOPE + ev, h * qd + MLA_NOPE + od,
                                            np.full(LANES - qd, zq)]) for h in range(MLA_HEADS)])
    wuq = jnp.take(jnp.concatenate([prm["mla_w_uq"], jnp.zeros((MLA_Q_LORA, 1), F32)], axis=1), qcols, axis=1).astype(BF16)
    kvd = MLA_NOPE + MLA_V
    zk = MLA_HEADS * kvd
    kcols = np.concatenate([np.concatenate([h * kvd + np.arange(MLA_NOPE), np.full(LANES - MLA_NOPE, zk)])
                            for h in range(MLA_HEADS)])
    vcols = np.concatenate([np.concatenate([h * kvd + MLA_NOPE + np.arange(MLA_V), np.full(LANES - MLA_V, zk)])
                            for h in range(MLA_HEADS)])
    wukv = jnp.concatenate([prm["mla_w_ukv"], jnp.zeros((MLA_KV_LORA, 1), F32)], axis=1)
    wuk = jnp.take(wukv, kcols, axis=1).astype(BF16)
    wuv = jnp.take(wukv, vcols, axis=1).astype(BF16)

    def row(v):
        return jnp.pad(v.astype(F32), (0, HEADS_W - v.shape[0]))

    rope_perm = np.concatenate([MLA_NOPE + ev, MLA_NOPE + od])
    pad_q = jnp.zeros((LANES - qd,), F32)
    kg, qg = prm["mla_k_gain"], prm["mla_q_gain"]
    q_scale = (qd ** -0.5) * LOG2E
    inv_cnt = np.tile(np.concatenate([np.full(MLA_NOPE, 1.0 / MLA_NOPE), np.full(MLA_ROPE, 1.0 / MLA_ROPE),
                                      np.ones(LANES - qd)]), MLA_HEADS).astype(np.float32)
    ones_lane = np.tile((np.arange(LANES) == MLA_V).astype(np.float32), MLA_HEADS)
    vec = jnp.stack([
        row(prm["mix_norm"]),
        row(prm["mla_kv_norm"]),
        row(prm["mla_q_norm"]),
        jnp.tile(jnp.concatenate([kg[:MLA_NOPE], jnp.zeros((LANES - MLA_NOPE,), F32)]), MLA_HEADS),
        row(jnp.concatenate([jnp.zeros((MLA_NOPE,), F32), kg[rope_perm], pad_q])),
        jnp.tile(jnp.concatenate([qg[:MLA_NOPE], qg[rope_perm], pad_q]), MLA_HEADS) * q_scale,
        jnp.asarray(inv_cnt),
        row(prm["conv_glu_b"]),
        jnp.asarray(ones_lane),
    ] + [jnp.zeros((HEADS_W,), F32)] * 7)

    seg = np.arange(MXU_DIM)
    seg_id = (seg // LANES) * 4 + np.where(seg % LANES < MLA_NOPE, 0, np.where(seg % LANES < qd, 1, 2))
    gmat = jnp.asarray((seg_id[:, None] == seg_id[None, :]).astype(np.float32)).astype(BF16)
    return win, wuk, wuv, wuq, vec, gmat


def _mla_rope_tables(n_tokens, use_rope):
    ones = jnp.ones((n_tokens, MLA_NOPE), F32)
    tail = jnp.ones((n_tokens, LANES - MLA_NOPE - MLA_ROPE), F32)
    if not use_rope:
        c = jnp.ones((n_tokens, LANES), F32)
        return c, jnp.zeros_like(c)
    cos, sin = _axial_angles(n_tokens, MLA_ROPE)
    c = jnp.concatenate([ones, cos, cos, tail], axis=-1)
    s = jnp.concatenate([0 * ones, -sin, sin, 0 * tail], axis=-1)
    return c, s


def _axial_angles(n_tokens, dim):
    t = jnp.arange(n_tokens)
    row = (t // GRID_W).astype(F32)
    col = (t % GRID_W).astype(F32)
    n_pairs = dim // 4
    inv = ROPE_BASE ** (-jnp.arange(n_pairs, dtype=F32) / n_pairs)
    ang = jnp.concatenate([row[:, None] * inv, col[:, None] * inv], axis=-1)
    return jnp.cos(ang), jnp.sin(ang)


def _even_proj(x, mod2, weights, use_rope):
    win, wuk, wuv, wuq, vec, gmat = weights
    b, l, d = x.shape
    tm = min(TOKEN_TILE, l)
    cos, sin = _mla_rope_tables(l, use_rope)
    tok = lambda w: pl.BlockSpec((1, tm, w), lambda i, j: (i, j, 0))
    return pl.pallas_call(
        _even_proj_kernel,
        grid=(b, l // tm),
        in_specs=[tok(d),
                  pl.BlockSpec((1, 2, d), lambda i, j: (i, 0, 0)),
                  _resident(vec.shape), _resident(win.shape), _resident(wuk.shape), _resident(wuv.shape),
                  _resident(wuq.shape), _resident(gmat.shape),
                  pl.BlockSpec((tm, LANES), lambda i, j: (j, 0)),
                  pl.BlockSpec((tm, LANES), lambda i, j: (j, 0))],
        out_specs=[tok(HEADS_W), tok(HEADS_W), tok(HEADS_W), tok(CONV_CH)],
        out_shape=[jax.ShapeDtypeStruct((b, l, HEADS_W), BF16)] * 3 + [jax.ShapeDtypeStruct((b, l, CONV_CH), F32)],
        compiler_params=_params("parallel", "parallel"),
    )(x, mod2, vec, win, wuk, wuv, wuq, gmat, cos, sin)


O_CK = 0
O_CV = O_CK + GQA_KV_HEADS * HEAD_DIM
O_NK = O_CV + GQA_KV_HEADS * LANES
O_NV = O_NK + NA_HEADS * HEAD_DIM
O_CQ = O_NV + NA_HEADS * LANES
O_NQ = O_CQ + GQA_HEADS * HEAD_DIM
O_COLS = O_NQ + NA_HEADS * HEAD_DIM
GQA_ORDER = (0, 4, 1, 5, 2, 6, 3, 7)


def _rope64(x, cos, sin):
    outs = []
    for p in range(x.shape[-1] // LANES):
        xs = x[:, p * LANES:(p + 1) * LANES]
        outs.append(xs * cos + _swap_halves(xs, HEAD_DIM // 2, 0) * sin)
    return outs[0] if len(outs) == 1 else jnp.concatenate(outs, axis=-1)


def _odd_proj_kernel(x_ref, mod_ref, vec_ref, win_ref, g_ref, c_ref, s_ref,
                     cq_ref, ck_ref, cv_ref, nq_ref, nk_ref, nv_ref):
    x = x_ref[0]
    mod = mod_ref[0]
    h = _rms_rows(x) * vec_ref[0:1, :]
    h = (h * (1.0 + mod[1:2]) + mod[0:1]).astype(BF16)
    pp = jnp.dot(h, win_ref[...], preferred_element_type=F32)
    cos = c_ref[...]
    sin = s_ref[...]
    g = g_ref[...]
    inv = 1.0 / HEAD_DIM

    def head_norm(v, gain):
        if v.shape[-1] == LANES:
            sq = v * v
            hi = sq.astype(BF16)
            lo = (sq - hi.astype(F32)).astype(BF16)
            gs = g[0:LANES, 0:LANES]
            ss = jnp.dot(hi, gs, preferred_element_type=F32) + jnp.dot(lo, gs, preferred_element_type=F32)
        else:
            ss = _seg_sumsq(v, g)
        return v * lax.rsqrt(ss * inv + EPS) * gain

    wq = GQA_HEADS * HEAD_DIM
    ck = head_norm(pp[:, O_CK:O_CV], vec_ref[1:2, 0:LANES])
    ck_ref[0] = _rope64(ck, cos, sin).astype(BF16)
    cv_ref[0] = (pp[:, O_CV:O_NK] + vec_ref[5:6, 0:GQA_KV_HEADS * LANES]).astype(BF16)
    nk_ref[0] = head_norm(pp[:, O_NK:O_NV], vec_ref[2:3, 0:wq]).astype(BF16)
    nv_ref[0] = (pp[:, O_NV:O_CQ] + vec_ref[5:6, :]).astype(BF16)
    cq = head_norm(pp[:, O_CQ:O_NQ], vec_ref[3:4, 0:wq])
    cq_ref[0] = _rope64(cq, cos, sin).astype(BF16)
    nq_ref[0] = head_norm(pp[:, O_NQ:O_COLS], vec_ref[4:5, 0:wq]).astype(BF16)


def _odd_weights(prm):
    d = D_MODEL
    ev = np.arange(0, HEAD_DIM, 2)
    od = np.arange(1, HEAD_DIM, 2)
    kvw = GQA_KV_HEADS * HEAD_DIM
    naw = NA_HEADS * HEAD_DIM
    zero_col = prm["w_in"].shape[1]
    pad64 = np.full(LANES - HEAD_DIM, zero_col)
    q0 = 2 * kvw + 2 * naw
    cols = np.concatenate(
        [np.concatenate([g * HEAD_DIM + ev, g * HEAD_DIM + od]) for g in range(GQA_KV_HEADS)]
        + [np.concatenate([kvw + g * HEAD_DIM + np.arange(HEAD_DIM), pad64]) for g in range(GQA_KV_HEADS)]
        + [2 * kvw + np.arange(naw)]
        + [np.concatenate([2 * kvw + naw + h * HEAD_DIM + np.arange(HEAD_DIM), pad64]) for h in range(NA_HEADS)]
        + [np.concatenate([q0 + h * HEAD_DIM + ev, q0 + h * HEAD_DIM + od]) for h in GQA_ORDER]
        + [q0 + GQA_HEADS * HEAD_DIM + np.arange(naw)])
    assert cols.shape[0] == O_COLS
    w_ext = jnp.concatenate([prm["w_in"], jnp.zeros((d, 1), F32)], axis=1)
    win = jnp.take(w_ext, cols, axis=1).astype(BF16)

    def row(v):
        return jnp.pad(v.astype(F32), (0, HEADS_W - v.shape[0]))

    perm = np.concatenate([ev, od])
    q_scale = (HEAD_DIM ** -0.5) * LOG2E
    ones_lane = np.tile((np.arange(LANES) == HEAD_DIM).astype(np.float32), NA_HEADS)
    vec = jnp.stack([
        row(prm["mix_norm"]),
        row(jnp.tile(prm["gqa_k_gain"][perm], GQA_KV_HEADS)),
        row(jnp.tile(prm["na_k_gain"], NA_HEADS)),
        row(jnp.tile(prm["gqa_q_gain"][perm], GQA_HEADS) * q_scale),
        row(jnp.tile(prm["na_q_gain"], NA_HEADS) * q_scale),
        jnp.asarray(ones_lane),
    ] + [jnp.zeros((HEADS_W,), F32)] * 2)
    seg_id = np.arange(MXU_DIM) // HEAD_DIM
    gmat = jnp.asarray((seg_id[:, None] == seg_id[None, :]).astype(np.float32)).astype(BF16)
    return win, vec, gmat


def _gqa_rope_tables(n_tokens, use_rope):
    if not use_rope:
        c = jnp.ones((n_tokens, LANES), F32)
        return c, jnp.zeros_like(c)
    cos, sin = _axial_angles(n_tokens, HEAD_DIM)
    return jnp.concatenate([cos, cos, cos, cos], axis=-1), jnp.concatenate([-sin, sin, -sin, sin], axis=-1)


def _odd_proj(x, mod2, weights, use_rope):
    win, vec, gmat = weights
    b, l, d = x.shape
    tm = min(TOKEN_TILE, l)
    cos, sin = _gqa_rope_tables(l, use_rope)
    tok = lambda w: pl.BlockSpec((1, tm, w), lambda i, j: (i, j, 0))
    widths = (GQA_HEADS * HEAD_DIM, GQA_KV_HEADS * HEAD_DIM, GQA_KV_HEADS * LANES,
              NA_HEADS * HEAD_DIM, NA_HEADS * HEAD_DIM, NA_HEADS * LANES)
    return pl.pallas_call(
        _odd_proj_kernel,
        grid=(b, l // tm),
        in_specs=[tok(d),
                  pl.BlockSpec((1, 2, d), lambda i, j: (i, 0, 0)),
                  _resident(vec.shape), _resident(win.shape), _resident(gmat.shape),
                  pl.BlockSpec((tm, LANES), lambda i, j: (j, 0)),
                  pl.BlockSpec((tm, LANES), lambda i, j: (j, 0))],
        out_specs=[tok(w) for w in widths],
        out_shape=[jax.ShapeDtypeStruct((b, l, w), BF16) for w in widths],
        compiler_params=_params("parallel", "parallel"),
    )(x, mod2, vec, win, gmat, cos, sin)


def _pair_queries(q, packed):
    if not packed:
        return q[:, 0:LANES], q[:, LANES:2 * LANES]
    lane = lax.broadcasted_iota(jnp.int32, q.shape, 1)
    zero = jnp.zeros_like(q)
    return jnp.where(lane < HEAD_DIM, q, zero), jnp.where(lane >= HEAD_DIM, q, zero)


def _pair_output(acc_a, acc_b):
    oa = acc_a / acc_a[:, HEAD_DIM:HEAD_DIM + 1]
    ob = acc_b / acc_b[:, HEAD_DIM:HEAD_DIM + 1]
    lane = lax.broadcasted_iota(jnp.int32, oa.shape, 1)
    return jnp.where(lane < HEAD_DIM, oa, pltpu.roll(ob, HEAD_DIM, 1))


def _score(q, k):
    return lax.dot_general(q, k, (((1,), (1,)), ((), ())), preferred_element_type=F32)


def _attn_kernel(*refs, n_src, packed, src_len):
    q_ref = refs[0]
    kv_refs = refs[1:1 + 2 * n_src]
    o_ref = refs[1 + 2 * n_src]
    qs = _pair_queries(q_ref[0], packed)
    rows = qs[0].shape[0]
    m = [jnp.full((rows, 1), NEG_BIG, F32) for _ in range(2)]
    acc = [jnp.zeros((rows, LANES), F32) for _ in range(2)]
    for s in range(n_src):
        k_ref, v_ref = kv_refs[2 * s], kv_refs[2 * s + 1]
        tk = min(ATT_TK, src_len[s])
        for blk in range(src_len[s] // tk):
            rs = slice(blk * tk, (blk + 1) * tk)
            for hd in range(2):
                ks = slice(0, LANES) if packed else slice(hd * LANES, (hd + 1) * LANES)
                sc = _score(qs[hd], k_ref[0, rs, ks])
                m_new = jnp.maximum(m[hd], jnp.max(sc, axis=-1, keepdims=True))
                p = jnp.exp2(sc - m_new).astype(BF16)
                pv = jnp.dot(p, v_ref[0, rs, hd * LANES:(hd + 1) * LANES], preferred_element_type=F32)
                acc[hd] = acc[hd] * jnp.exp2(m[hd] - m_new) + pv
                m[hd] = m_new
    o_ref[0] = _pair_output(acc[0], acc[1]).astype(o_ref.dtype)


def _pair_attention(q, sources, packed):
    b, lq, qw = q.shape
    wq = LANES if packed else 2 * LANES
    n_pairs = qw // wq
    tq = min(ATT_TQ, lq)
    in_specs = [pl.BlockSpec((1, tq, wq), lambda i, p, j: (i, j, p))]
    args = [q]
    for k, v in sources:
        lk = k.shape[1]
        if packed:
            in_specs.append(pl.BlockSpec((1, lk, LANES), lambda i, p, j: (i, 0, 0)))
            in_specs.append(pl.BlockSpec((1, lk, 2 * LANES), lambda i, p, j: (i, 0, 0)))
        else:
            in_specs.append(pl.BlockSpec((1, lk, 2 * LANES), lambda i, p, j: (i, 0, p)))
            in_specs.append(pl.BlockSpec((1, lk, 2 * LANES), lambda i, p, j: (i, 0, p)))
        args += [k, v]
    kern = functools.partial(_attn_kernel, n_src=len(sources), packed=packed,
                             src_len=tuple(k.shape[1] for k, _ in sources))
    return pl.pallas_call(
        kern,
        grid=(b, n_pairs, lq // tq),
        in_specs=in_specs,
        out_specs=pl.BlockSpec((1, tq, LANES), lambda i, p, j: (i, j, p)),
        out_shape=jax.ShapeDtypeStruct((b, lq, n_pairs * LANES), BF16),
        compiler_params=_params("parallel", "parallel", "arbitrary"),
    )(*args)


def _na_ctx_kernel(q_ref, k_ref, v_ref, o_ref):
    qs = _pair_queries(q_ref[0], True)
    accs = []
    for hd in range(2):
        sc = _score(qs[hd], k_ref[0])
        p = jnp.exp2(sc - jnp.max(sc, axis=-1, keepdims=True)).astype(BF16)
        accs.append(jnp.dot(p, v_ref[0, :, hd * LANES:(hd + 1) * LANES], preferred_element_type=F32))
    o_ref[0] = _pair_output(accs[0], accs[1]).astype(o_ref.dtype)


def _na_ctx_attention(q, k, v):
    b, l, w = q.shape
    n_pairs = w // LANES
    return pl.pallas_call(
        _na_ctx_kernel,
        grid=(b, n_pairs),
        in_specs=[pl.BlockSpec((1, l, LANES), lambda i, p: (i, 0, p)),
                  pl.BlockSpec((1, l, LANES), lambda i, p: (i, 0, p)),
                  pl.BlockSpec((1, l, 2 * LANES), lambda i, p: (i, 0, p))],
        out_specs=pl.BlockSpec((1, l, LANES), lambda i, p: (i, 0, p)),
        out_shape=jax.ShapeDtypeStruct((b, l, w), BF16),
        compiler_params=_params("parallel", "parallel"),
    )(q, k, v)


def _na_kernel(q_ref, k_ref, v_ref, kc_ref, vc_ref, bias_ref, o_ref, *, n_blocks, grid_rows):
    blk_q = NA_ROWS * GRID_W
    win = NA_WIN_ROWS * GRID_W
    kc = kc_ref[0]

    def body(i, carry):
        q0 = pl.multiple_of(i * blk_q, blk_q)
        ws = jnp.clip(i * NA_ROWS - WIN_H // 2, 0, grid_rows - NA_WIN_ROWS)
        k0 = pl.multiple_of(ws * GRID_W, GRID_W)
        cfg = jnp.where(i == 0, 0, jnp.where(i == n_blocks - 1, 2, 1))
        qs = _pair_queries(q_ref[0, pl.ds(q0, blk_q), :], True)
        kw = k_ref[0, pl.ds(k0, win), :]
        accs = []
        for hd in range(2):
            s_nb = _score(qs[hd], kw) + bias_ref[0, cfg, hd]
            s_c = _score(qs[hd], kc)
            m = jnp.maximum(jnp.max(s_nb, axis=-1, keepdims=True), jnp.max(s_c, axis=-1, keepdims=True))
            p_nb = jnp.exp2(s_nb - m).astype(BF16)
            p_c = jnp.exp2(s_c - m).astype(BF16)
            vs = slice(hd * LANES, (hd + 1) * LANES)
            accs.append(jnp.dot(p_nb, v_ref[0, pl.ds(k0, win), vs], preferred_element_type=F32)
                        + jnp.dot(p_c, vc_ref[0, :, vs], preferred_element_type=F32))
        o_ref[0, pl.ds(q0, blk_q), :] = _pair_output(accs[0], accs[1]).astype(o_ref.dtype)
        return carry

    lax.fori_loop(0, n_blocks, body, 0)


def _na_bias_table(rpb, grid_rows):
    n_blocks = grid_rows // NA_ROWS
    assert grid_rows % NA_ROWS == 0 and grid_rows >= NA_WIN_ROWS + NA_ROWS
    ri = np.arange(NA_ROWS)[:, None, None, None]
    cq = np.arange(GRID_W)[None, :, None, None]
    kj = np.arange(NA_WIN_ROWS)[None, None, :, None]
    ck = np.arange(GRID_W)[None, None, None, :]
    tabs = []
    for blk in (0, 1, n_blocks - 1):
        ws = int(np.clip(blk * NA_ROWS - WIN_H // 2, 0, grid_rows - NA_WIN_ROWS))
        r = blk * NA_ROWS + ri
        krow = ws + kj
        rs = np.clip(r - WIN_H // 2, 0, grid_rows - WIN_H)
        cs = np.clip(cq - WIN_W // 2, 0, GRID_W - WIN_W)
        valid = (krow >= rs) & (krow < rs + WIN_H) & (ck >= cs) & (ck < cs + WIN_W)
        dr = np.clip(krow - r + (WIN_H - 1), 0, 2 * WIN_H - 2)
        dc = np.clip(ck - cq + (WIN_W - 1), 0, 2 * WIN_W - 2)
        shape = (NA_ROWS, GRID_W, NA_WIN_ROWS, GRID_W)
        dr, dc, valid = (np.broadcast_to(a, shape).reshape(NA_ROWS * GRID_W, NA_WIN_ROWS * GRID_W) for a in (dr, dc, valid))
        tabs.append(jnp.where(jnp.asarray(valid)[None], rpb[:, dr, dc] * LOG2E, NEG_BIG))
    tab = jnp.stack(tabs, axis=1)
    h = tab.shape[0]
    return tab.reshape(h // 2, 2, 3, *tab.shape[2:]).transpose(0, 2, 1, 3, 4)


def _neighbourhood_attention(q, k, v, kc, vc, bias):
    b, s, w = q.shape
    n_pairs = w // LANES
    grid_rows = s // GRID_W
    n_blocks = grid_rows // NA_ROWS
    lc = kc.shape[1]
    return pl.pallas_call(
        functools.partial(_na_kernel, n_blocks=n_blocks, grid_rows=grid_rows),
        grid=(b, n_pairs),
        in_specs=[pl.BlockSpec((1, s, LANES), lambda i, p: (i, 0, p)),
                  pl.BlockSpec((1, s, LANES), lambda i, p: (i, 0, p)),
                  pl.BlockSpec((1, s, 2 * LANES), lambda i, p: (i, 0, p)),
                  pl.BlockSpec((1, lc, LANES), lambda i, p: (i, 0, p)),
                  pl.BlockSpec((1, lc, 2 * LANES), lambda i, p: (i, 0, p)),
                  pl.BlockSpec((1,) + bias.shape[1:], lambda i, p: (p, 0, 0, 0, 0))],
        out_specs=pl.BlockSpec((1, s, LANES), lambda i, p: (i, 0, p)),
        out_shape=jax.ShapeDtypeStruct((b, s, w), BF16),
        compiler_params=_params("parallel", "arbitrary"),
    )(q, k, v, kc, vc, bias)


def _conv_kernel(y_ref, w_ref, vec_ref, o_ref, pad_ref, *, seq):
    ch = y_ref.shape[-1]
    zeros = jnp.zeros((CONV_HALO, ch), F32)
    pad_ref[0:CONV_HALO, :] = zeros
    pad_ref[CONV_HALO + seq:2 * CONV_HALO + seq, :] = zeros
    pad_ref[CONV_HALO:CONV_HALO + seq, :] = y_ref[0]
    first = CONV_HALO - CONV_WIDTH // 2

    def body(i, carry):
        base = pl.multiple_of(i * CONV_TILE, CONV_TILE)
        window = pad_ref[pl.ds(base, CONV_TILE + 2 * CONV_HALO), :]
        acc = jnp.zeros((CONV_TILE, ch), F32)
        for t in range(CONV_WIDTH):
            acc = acc + window[first + t:first + t + CONV_TILE, :] * w_ref[t:t + 1, :]
        acc = acc + vec_ref[0:1, :]
        mu = jnp.mean(acc, axis=-1, keepdims=True)
        cen = acc - mu
        var = jnp.mean(cen * cen, axis=-1, keepdims=True)
        z = cen * lax.rsqrt(var + EPS) * vec_ref[1:2, :] + vec_ref[2:3, :]
        o_ref[0, pl.ds(base, CONV_TILE), :] = _silu(z).astype(o_ref.dtype)
        return carry

    lax.fori_loop(0, seq // CONV_TILE, body, 0)


def _conv_module(y, prm):
    b, l, ch = y.shape
    vec = jnp.stack([prm["conv_dw_b"], prm["conv_ln_g"], prm["conv_ln_b"]] + [jnp.zeros((ch,), F32)] * 5)
    w = jnp.pad(prm["conv_dw_w"], ((0, 1), (0, 0)))
    return pl.pallas_call(
        functools.partial(_conv_kernel, seq=l),
        grid=(b,),
        in_specs=[pl.BlockSpec((1, l, ch), lambda i: (i, 0, 0)), _resident(w.shape), _resident(vec.shape)],
        out_specs=pl.BlockSpec((1, l, ch), lambda i: (i, 0, 0)),
        out_shape=jax.ShapeDtypeStruct((b, l, ch), BF16),
        scratch_shapes=[pltpu.VMEM((l + 2 * CONV_HALO, ch), F32)],
        compiler_params=_params("parallel"),
    )(y, w, vec)


def _out_kernel(x_ref, a_ref, b_ref, gate_ref, wa_ref, wb_ref, o_ref):
    y = (jnp.dot(a_ref[0], wa_ref[...], preferred_element_type=F32)
         + jnp.dot(b_ref[0], wb_ref[...], preferred_element_type=F32))
    o_ref[0] = x_ref[0] + gate_ref[0] * y


def _mixer_out(x, part_a, part_b, gate, w_a, w_b):
    b, l, d = x.shape
    tm = min(TOKEN_TILE, l)
    ha = part_a.shape[-1]
    hb = part_b.shape[-1]
    tok = lambda w: pl.BlockSpec((1, tm, w), lambda i, j: (i, j, 0))
    return pl.pallas_call(
        _out_kernel,
        grid=(b, l // tm),
        in_specs=[tok(d), tok(ha), tok(hb), pl.BlockSpec((1, 1, d), lambda i, j: (i, 0, 0)),
                  _resident((ha, d)), _resident((hb, d))],
        out_specs=tok(d),
        out_shape=jax.ShapeDtypeStruct((b, l, d), F32),
        compiler_params=_params("parallel", "parallel"),
    )(x, part_a, part_b, gate, w_a.astype(BF16), w_b.astype(BF16))


def _flat_ctx(a, batch):
    if a.shape[0] == batch:
        return a.reshape(1, batch * a.shape[1], a.shape[2])
    return a.reshape(batch, a.shape[1] // batch, a.shape[2])


def _trunk_layer(xl, xc, mods, prm, even, ctx_out):
    batch = xl.shape[0]
    mod_l, mod_c = mods
    xl = _half_ffn(xl, mod_l[:, 0:3], prm["ffn1_norm"], prm["ffn1_w_in"], prm["ffn1_w_out"])
    xc = _half_ffn(xc, mod_c[:, 0:3], prm["ffn1_norm"], prm["ffn1_w_in"], prm["ffn1_w_out"])
    w_out = prm["w_out"]
    half = w_out.shape[0] // 2
    if even:
        weights = _even_weights(prm)
        ql, kl, vl, yl = _even_proj(xl, mod_l[:, 3:5], weights, True)
        qc, kc, vc, yc = _even_proj(xc, mod_c[:, 3:5], weights, False)
        qc, kc, vc, yc = (_flat_ctx(a, batch) for a in (qc, kc, vc, yc))
        att_l = _pair_attention(ql, [(kc, vc), (kl, vl)], packed=False)
        conv_l = _conv_module(yl, prm)
        w_a, w_b = w_out[:half], w_out[half:]
        xl = _mixer_out(xl, att_l, conv_l, mod_l[:, 5:6], w_a, w_b)
        if ctx_out:
            att_c = _pair_attention(qc, [(kc, vc)], packed=False)
            conv_c = _conv_module(yc, prm)
            xc = _mixer_out(xc, _flat_ctx(att_c, batch), _flat_ctx(conv_c, batch), mod_c[:, 5:6], w_a, w_b)
    else:
        weights = _odd_weights(prm)
        cql, ckl, cvl, nql, nkl, nvl = _odd_proj(xl, mod_l[:, 3:5], weights, True)
        ctx_parts = [_flat_ctx(a, batch) for a in _odd_proj(xc, mod_c[:, 3:5], weights, False)]
        cqc, ckc, cvc, nqc, nkc, nvc = ctx_parts
        gqa_l = _pair_attention(cql, [(ckc, cvc), (ckl, cvl)], packed=True)
        bias = _na_bias_table(prm["na_rpb"], xl.shape[1] // GRID_W)
        na_l = _neighbourhood_attention(nql, nkl, nvl, nkc, nvc, bias)
        order = np.concatenate([h * HEAD_DIM + np.arange(HEAD_DIM) for h in GQA_ORDER])
        w_a, w_b = w_out[:half][order], w_out[half:]
        xl = _mixer_out(xl, gqa_l, na_l, mod_l[:, 5:6], w_a, w_b)
        if ctx_out:
            gqa_c = _pair_attention(cqc, [(ckc, cvc)], packed=True)
            na_c = _na_ctx_attention(nqc, nkc, nvc)
            xc = _mixer_out(xc, _flat_ctx(gqa_c, batch), _flat_ctx(na_c, batch), mod_c[:, 5:6], w_a, w_b)
    xl = _half_ffn(xl, mod_l[:, 6:9], prm["ffn2_norm"], prm["ffn2_w_in"], prm["ffn2_w_out"])
    if ctx_out:
        xc = _half_ffn(xc, mod_c[:, 6:9], prm["ffn2_norm"], prm["ffn2_w_in"], prm["ffn2_w_out"])
    return xl, xc


def _layer_mods(c, c_ctx, prm):
    batch, d = c.shape
    rows = -(-(batch + 1) // 8) * 8
    cond = jnp.zeros((rows, d), F32).at[:batch].set(c).at[batch].set(c_ctx)
    mod = _modulation(cond, prm["mod_w"], prm["mod_b"]).reshape(rows, N_MOD, d)
    return mod[:batch], mod[batch:batch + 1]


def kernel(x, c, ctx, c_ctx,
           l0_mod_w, l0_mod_b, l0_ffn1_norm, l0_ffn1_w_in, l0_ffn1_w_out, l0_mix_norm, l0_w_in,
           l0_mla_q_norm, l0_mla_w_uq, l0_mla_kv_norm, l0_mla_w_ukv, l0_mla_q_gain, l0_mla_k_gain,
           l0_conv_glu_b, l0_conv_dw_w, l0_conv_dw_b, l0_conv_ln_g, l0_conv_ln_b,
           l0_w_out, l0_ffn2_norm, l0_ffn2_w_in, l0_ffn2_w_out,
           l1_mod_w, l1_mod_b, l1_ffn1_norm, l1_ffn1_w_in, l1_ffn1_w_out, l1_mix_norm, l1_w_in,
           l1_gqa_q_gain, l1_gqa_k_gain, l1_na_q_gain, l1_na_k_gain, l1_na_rpb,
           l1_w_out, l1_ffn2_norm, l1_ffn2_w_in, l1_ffn2_w_out):
    layers = (
        dict(mod_w=l0_mod_w, mod_b=l0_mod_b, ffn1_norm=l0_ffn1_norm, ffn1_w_in=l0_ffn1_w_in,
             ffn1_w_out=l0_ffn1_w_out, mix_norm=l0_mix_norm, w_in=l0_w_in,
             mla_q_norm=l0_mla_q_norm, mla_w_uq=l0_mla_w_uq, mla_kv_norm=l0_mla_kv_norm,
             mla_w_ukv=l0_mla_w_ukv, mla_q_gain=l0_mla_q_gain, mla_k_gain=l0_mla_k_gain,
             conv_glu_b=l0_conv_glu_b, conv_dw_w=l0_conv_dw_w, conv_dw_b=l0_conv_dw_b,
             conv_ln_g=l0_conv_ln_g, conv_ln_b=l0_conv_ln_b, w_out=l0_w_out,
             ffn2_norm=l0_ffn2_norm, ffn2_w_in=l0_ffn2_w_in, ffn2_w_out=l0_ffn2_w_out),
        dict(mod_w=l1_mod_w, mod_b=l1_mod_b, ffn1_norm=l1_ffn1_norm, ffn1_w_in=l1_ffn1_w_in,
             ffn1_w_out=l1_ffn1_w_out, mix_norm=l1_mix_norm, w_in=l1_w_in,
             gqa_q_gain=l1_gqa_q_gain, gqa_k_gain=l1_gqa_k_gain, na_q_gain=l1_na_q_gain,
             na_k_gain=l1_na_k_gain, na_rpb=l1_na_rpb, w_out=l1_w_out,
             ffn2_norm=l1_ffn2_norm, ffn2_w_in=l1_ffn2_w_in, ffn2_w_out=l1_ffn2_w_out),
    )
    batch = x.shape[0]
    xl, xc = x, _flat_ctx(ctx, batch)
    for i, prm in enumerate(layers):
        mods = _layer_mods(c, c_ctx, prm)
        xl, xc = _trunk_layer(xl, xc, mods, prm, even=(i % 2 == 0), ctx_out=(i < len(layers) - 1))
    return xl
```

```python
import functools
import math

import numpy as np
import jax
import jax.numpy as jnp
from jax import lax
from jax.experimental import pallas as pl
from jax.experimental.pallas import tpu as pltpu

F32 = jnp.float32
BF16 = jnp.bfloat16

D_MODEL = 1024
CTX_LEN = 256
GRID_W = 64
D_FF = 2816
N_MOD = 9
EPS = 1e-6
ROPE_BASE = 10000.0
MLA_HEADS = 8
MLA_NOPE = 64
MLA_ROPE = 32
MLA_V = 64
MLA_Q_LORA = 384
MLA_KV_LORA = 256
CONV_CH = 512
CONV_WIDTH = 31
HEAD_DIM = 64
GQA_HEADS = 8
GQA_KV_HEADS = 2
NA_HEADS = 8
WIN_H = 8
WIN_W = 16

LANES = 128
SUBLANES = 8
MXU_DIM = 256
VMEM_LIMIT = 56 * 1024 * 1024
LOG2E = math.log2(math.e)
NEG_BIG = -1e30

FFN_CHUNK = 256
TOKEN_TILE = 512
ATT_TQ = 512
ATT_TK = 256
NA_ROWS = 4
NA_WIN_ROWS = 12
NA_UNROLL = 2
CONV_TILE = 128
CONV_HALO = 16


def _params(*sem):
    return pltpu.CompilerParams(dimension_semantics=sem, vmem_limit_bytes=VMEM_LIMIT)


def _resident(shape):
    return pl.BlockSpec(shape, lambda *_: (0,) * len(shape), pipeline_mode=pl.Buffered(1))


def _silu(x):
    return x * jax.nn.sigmoid(x)


def _rms_rows(x):
    return x * lax.rsqrt(jnp.mean(x * x, axis=-1, keepdims=True) + EPS)


def _seg_sumsq(x, g):
    outs = []
    for p in range(x.shape[-1] // MXU_DIM):
        sq = x[:, p * MXU_DIM:(p + 1) * MXU_DIM]
        sq = sq * sq
        hi = sq.astype(BF16)
        lo = (sq - hi.astype(F32)).astype(BF16)
        outs.append(jnp.dot(hi, g, preferred_element_type=F32) + jnp.dot(lo, g, preferred_element_type=F32))
    return outs[0] if len(outs) == 1 else jnp.concatenate(outs, axis=-1)


def _swap_halves(x, half, lo):
    lane = lax.broadcasted_iota(jnp.int32, x.shape, 1)
    first = ((lane - lo) & (2 * half - 1)) < half
    return jnp.where(first, pltpu.roll(x, LANES - half, 1), pltpu.roll(x, half, 1))


def _mod_kernel(c_ref, w_ref, b_ref, o_ref):
    a = _silu(c_ref[...]).astype(BF16)
    o_ref[...] = jnp.dot(a, w_ref[...].astype(BF16), preferred_element_type=F32) + b_ref[...]


def _modulation(cond, mod_w, mod_b):
    rows, d = cond.shape
    n = mod_w.shape[1]
    return pl.pallas_call(
        _mod_kernel,
        grid=(n // d,),
        in_specs=[pl.BlockSpec((rows, d), lambda j: (0, 0)),
                  pl.BlockSpec((d, d), lambda j: (0, j)),
                  pl.BlockSpec((1, d), lambda j: (0, j))],
        out_specs=pl.BlockSpec((rows, d), lambda j: (0, j)),
        out_shape=jax.ShapeDtypeStruct((rows, n), F32),
        compiler_params=_params("arbitrary"),
    )(cond, mod_w, mod_b.reshape(1, n))


def _ffn_kernel(x_ref, mod_ref, g_ref, win_ref, wout_ref, o_ref, h_ref, a_ref, *, n_chunks):
    x = x_ref[0]
    mod = mod_ref[0]
    h = _rms_rows(x) * g_ref[...]
    h_ref[...] = (h * (1.0 + mod[1:2]) + mod[0:1]).astype(BF16)
    for c in range(n_chunks):
        gate = jnp.dot(h_ref[...], win_ref[c], preferred_element_type=F32)
        up = jnp.dot(h_ref[...], win_ref[n_chunks + c], preferred_element_type=F32)
        a_ref[:, c * FFN_CHUNK:(c + 1) * FFN_CHUNK] = (_silu(gate) * up).astype(BF16)
    y = jnp.dot(a_ref[...], wout_ref[...], preferred_element_type=F32)
    o_ref[0] = x + (0.5 * mod[2:3]) * y


def _half_ffn(x, mod, norm_g, w_in, w_out):
    b, l, d = x.shape
    tm = min(TOKEN_TILE, l)
    n_chunks = D_FF // FFN_CHUNK
    win = w_in.astype(BF16).reshape(d, 2 * n_chunks, FFN_CHUNK).transpose(1, 0, 2)
    wout = w_out.astype(BF16)
    return pl.pallas_call(
        functools.partial(_ffn_kernel, n_chunks=n_chunks),
        grid=(b, l // tm),
        in_specs=[pl.BlockSpec((1, tm, d), lambda i, j: (i, j, 0)),
                  pl.BlockSpec((1, 3, d), lambda i, j: (i, 0, 0)),
                  _resident((1, d)),
                  _resident((2 * n_chunks, d, FFN_CHUNK)),
                  _resident((D_FF, d))],
        out_specs=pl.BlockSpec((1, tm, d), lambda i, j: (i, j, 0)),
        out_shape=jax.ShapeDtypeStruct((b, l, d), F32),
        scratch_shapes=[pltpu.VMEM((tm, d), BF16), pltpu.VMEM((tm, D_FF), BF16)],
        compiler_params=_params("parallel", "parallel"),
    )(x, mod, norm_g.reshape(1, d), win, wout)


E_KR = MLA_KV_LORA
E_CQ = E_KR + LANES
E_CA = E_CQ + MLA_Q_LORA
E_CG = E_CA + CONV_CH
E_COLS = E_CG + CONV_CH
HEADS_W = MLA_HEADS * LANES


def _even_proj_kernel(x_ref, mod_ref, vec_ref, win_ref, wuk_ref, wuv_ref, wuq_ref, g_ref, c_ref, s_ref,
                      q_ref, k_ref, v_ref, y_ref):
    x = x_ref[0]
    mod = mod_ref[0]
    h = _rms_rows(x) * vec_ref[0:1, :]
    h = (h * (1.0 + mod[1:2]) + mod[0:1]).astype(BF16)
    pp = jnp.dot(h, win_ref[...], preferred_element_type=F32)
    cos = c_ref[...]
    sin = s_ref[...]
    g = g_ref[...]

    ckv = (_rms_rows(pp[:, 0:E_KR]) * vec_ref[1:2, 0:MLA_KV_LORA]).astype(BF16)
    kn = jnp.dot(ckv, wuk_ref[...], preferred_element_type=F32)
    vv = jnp.dot(ckv, wuv_ref[...], preferred_element_type=F32)
    kn = kn * lax.rsqrt(_seg_sumsq(kn, g) * vec_ref[6:7, :] + EPS) * vec_ref[3:4, :]
    kr = pp[:, E_KR:E_CQ]
    kr = kr * lax.rsqrt(jnp.sum(kr * kr, axis=-1, keepdims=True) * (1.0 / MLA_ROPE) + EPS) * vec_ref[4:5, 0:LANES]
    kr = kr * cos + _swap_halves(kr, MLA_ROPE // 2, MLA_NOPE) * sin
    v_ref[0] = (vv + vec_ref[8:9, :]).astype(BF16)

    cq = (_rms_rows(pp[:, E_CQ:E_CA]) * vec_ref[2:3, 0:MLA_Q_LORA]).astype(BF16)
    q = jnp.dot(cq, wuq_ref[...], preferred_element_type=F32)
    q = q * lax.rsqrt(_seg_sumsq(q, g) * vec_ref[6:7, :] + EPS) * vec_ref[5:6, :]
    for hd in range(MLA_HEADS):
        sl = slice(hd * LANES, (hd + 1) * LANES)
        k_ref[0, :, sl] = (kn[:, sl] + kr).astype(BF16)
        qh = q[:, sl]
        qh = qh * cos + _swap_halves(qh, MLA_ROPE // 2, MLA_NOPE) * sin
        q_ref[0, :, sl] = qh.astype(BF16)

    glu_b = vec_ref[7:8, :]
    y_ref[0] = (pp[:, E_CA:E_CG] + glu_b[:, 0:CONV_CH]) * jax.nn.sigmoid(pp[:, E_CG:E_COLS] + glu_b[:, CONV_CH:])


def _even_weights(prm):
    d = D_MODEL
    ev = np.arange(0, MLA_ROPE, 2)
    od = np.arange(1, MLA_ROPE, 2)
    zero_col = prm["w_in"].shape[1]
    cols = np.concatenate([
        np.arange(MLA_KV_LORA),
        np.full(MLA_NOPE, zero_col), MLA_KV_LORA + ev, MLA_KV_LORA + od, np.full(LANES - MLA_NOPE - MLA_ROPE, zero_col),
        np.arange(MLA_KV_LORA + MLA_ROPE, zero_col)])
    w_ext = jnp.concatenate([prm["w_in"], jnp.zeros((d, 1), F32)], axis=1)
    win = jnp.take(w_ext, cols, axis=1).astype(BF16)

    qd = MLA_NOPE + MLA_ROPE
    zq = MLA_HEADS * qd
    qcols = np.concatenate([np.concatenate([h * qd + np.arange(MLA_NOPE), h * qd + MLA_NOPE + ev, h * qd + MLA_NOPE + od,
                                            np.full(LANES - qd, zq)]) for h in range(MLA_HEADS)])
    wuq = jnp.take(jnp.concatenate([prm["mla_w_uq"], jnp.zeros((MLA_Q_LORA, 1), F32)], axis=1), qcols, axis=1).astype(BF16)
    kvd = MLA_NOPE + MLA_V
    zk = MLA_HEADS * kvd
    kcols = np.concatenate([np.concatenate([h * kvd + np.arange(MLA_NOPE), np.full(LANES - MLA_NOPE, zk)])
                            for h in range(MLA_HEADS)])
    vcols = np.concatenate([np.concatenate([h * kvd + MLA_NOPE + np.arange(MLA_V), np.full(LANES - MLA_V, zk)])
                            for h in range(MLA_HEADS)])
    wukv = jnp.concatenate([prm["mla_w_ukv"], jnp.zeros((MLA_KV_LORA, 1), F32)], axis=1)
    wuk = jnp.take(wukv, kcols, axis=1).astype(BF16)
    wuv = jnp.take(wukv, vcols, axis=1).astype(BF16)

    def row(v):
        return jnp.pad(v.astype(F32), (0, HEADS_W - v.shape[0]))

    rope_perm = np.concatenate([MLA_NOPE + ev, MLA_NOPE + od])
    pad_q = jnp.zeros((LANES - qd,), F32)
    kg, qg = prm["mla_k_gain"], prm["mla_q_gain"]
    q_scale = (qd ** -0.5) * LOG2E
    inv_cnt = np.tile(np.concatenate([np.full(MLA_NOPE, 1.0 / MLA_NOPE), np.full(MLA_ROPE, 1.0 / MLA_ROPE),
                                      np.ones(LANES - qd)]), MLA_HEADS).astype(np.float32)
    ones_lane = np.tile((np.arange(LANES) == MLA_V).astype(np.float32), MLA_HEADS)
    vec = jnp.stack([
        row(prm["mix_norm"]),
        row(prm["mla_kv_norm"]),
        row(prm["mla_q_norm"]),
        jnp.tile(jnp.concatenate([kg[:MLA_NOPE], jnp.zeros((LANES - MLA_NOPE,), F32)]), MLA_HEADS),
        row(jnp.concatenate([jnp.zeros((MLA_NOPE,), F32), kg[rope_perm], pad_q])),
        jnp.tile(jnp.concatenate([qg[:MLA_NOPE], qg[rope_perm], pad_q]), MLA_HEADS) * q_scale,
        jnp.asarray(inv_cnt),
        row(prm["conv_glu_b"]),
        jnp.asarray(ones_lane),
    ] + [jnp.zeros((HEADS_W,), F32)] * 7)

    seg = np.arange(MXU_DIM)
    seg_id = (seg // LANES) * 4 + np.where(seg % LANES < MLA_NOPE, 0, np.where(seg % LANES < qd, 1, 2))
    gmat = jnp.asarray((seg_id[:, None] == seg_id[None, :]).astype(np.float32)).astype(BF16)
    return win, wuk, wuv, wuq, vec, gmat


def _mla_rope_tables(n_tokens, use_rope):
    ones = jnp.ones((n_tokens, MLA_NOPE), F32)
    tail = jnp.ones((n_tokens, LANES - MLA_NOPE - MLA_ROPE), F32)
    if not use_rope:
        c = jnp.ones((n_tokens, LANES), F32)
        return c, jnp.zeros_like(c)
    cos, sin = _axial_angles(n_tokens, MLA_ROPE)
    c = jnp.concatenate([ones, cos, cos, tail], axis=-1)
    s = jnp.concatenate([0 * ones, -sin, sin, 0 * tail], axis=-1)
    return c, s


def _axial_angles(n_tokens, dim):
    t = jnp.arange(n_tokens)
    row = (t // GRID_W).astype(F32)
    col = (t % GRID_W).astype(F32)
    n_pairs = dim // 4
    inv = ROPE_BASE ** (-jnp.arange(n_pairs, dtype=F32) / n_pairs)
    ang = jnp.concatenate([row[:, None] * inv, col[:, None] * inv], axis=-1)
    return jnp.cos(ang), jnp.sin(ang)


def _even_proj(x, mod2, weights, use_rope):
    win, wuk, wuv, wuq, vec, gmat = weights
    b, l, d = x.shape
    tm = min(TOKEN_TILE, l)
    cos, sin = _mla_rope_tables(l, use_rope)
    tok = lambda w: pl.BlockSpec((1, tm, w), lambda i, j: (i, j, 0))
    return pl.pallas_call(
        _even_proj_kernel,
        grid=(b, l // tm),
        in_specs=[tok(d),
                  pl.BlockSpec((1, 2, d), lambda i, j: (i, 0, 0)),
                  _resident(vec.shape), _resident(win.shape), _resident(wuk.shape), _resident(wuv.shape),
                  _resident(wuq.shape), _resident(gmat.shape),
                  pl.BlockSpec((tm, LANES), lambda i, j: (j, 0)),
                  pl.BlockSpec((tm, LANES), lambda i, j: (j, 0))],
        out_specs=[tok(HEADS_W), tok(HEADS_W), tok(HEADS_W), tok(CONV_CH)],
        out_shape=[jax.ShapeDtypeStruct((b, l, HEADS_W), BF16)] * 3 + [jax.ShapeDtypeStruct((b, l, CONV_CH), F32)],
        compiler_params=_params("parallel", "parallel"),
    )(x, mod2, vec, win, wuk, wuv, wuq, gmat, cos, sin)


O_CK = 0
O_CV = O_CK + GQA_KV_HEADS * HEAD_DIM
O_NK = O_CV + GQA_KV_HEADS * LANES
O_NV = O_NK + NA_HEADS * HEAD_DIM
O_CQ = O_NV + NA_HEADS * LANES
O_NQ = O_CQ + GQA_HEADS * HEAD_DIM
O_COLS = O_NQ + NA_HEADS * HEAD_DIM
GQA_ORDER = (0, 4, 1, 5, 2, 6, 3, 7)


def _rope64(x, cos, sin):
    outs = []
    for p in range(x.shape[-1] // LANES):
        xs = x[:, p * LANES:(p + 1) * LANES]
        outs.append(xs * cos + _swap_halves(xs, HEAD_DIM // 2, 0) * sin)
    return outs[0] if len(outs) == 1 else jnp.concatenate(outs, axis=-1)


def _odd_proj_kernel(x_ref, mod_ref, vec_ref, win_ref, g_ref, c_ref, s_ref,
                     cq_ref, ck_ref, cv_ref, nq_ref, nk_ref, nv_ref):
    x = x_ref[0]
    mod = mod_ref[0]
    h = _rms_rows(x) * vec_ref[0:1, :]
    h = (h * (1.0 + mod[1:2]) + mod[0:1]).astype(BF16)
    pp = jnp.dot(h, win_ref[...], preferred_element_type=F32)
    cos = c_ref[...]
    sin = s_ref[...]
    g = g_ref[...]
    inv = 1.0 / HEAD_DIM

    def head_norm(v, gain):
        if v.shape[-1] == LANES:
            sq = v * v
            hi = sq.astype(BF16)
            lo = (sq - hi.astype(F32)).astype(BF16)
            gs = g[0:LANES, 0:LANES]
            ss = jnp.dot(hi, gs, preferred_element_type=F32) + jnp.dot(lo, gs, preferred_element_type=F32)
        else:
            ss = _seg_sumsq(v, g)
        return v * lax.rsqrt(ss * inv + EPS) * gain

    wq = GQA_HEADS * HEAD_DIM
    ck = head_norm(pp[:, O_CK:O_CV], vec_ref[1:2, 0:LANES])
    ck_ref[0] = _rope64(ck, cos, sin).astype(BF16)
    cv_ref[0] = (pp[:, O_CV:O_NK] + vec_ref[5:6, 0:GQA_KV_HEADS * LANES]).astype(BF16)
    nk_ref[0] = head_norm(pp[:, O_NK:O_NV], vec_ref[2:3, 0:wq]).astype(BF16)
    nv_ref[0] = (pp[:, O_NV:O_CQ] + vec_ref[5:6, :]).astype(BF16)
    cq = head_norm(pp[:, O_CQ:O_NQ], vec_ref[3:4, 0:wq])
    cq_ref[0] = _rope64(cq, cos, sin).astype(BF16)
    nq_ref[0] = head_norm(pp[:, O_NQ:O_COLS], vec_ref[4:5, 0:wq]).astype(BF16)


def _odd_weights(prm):
    d = D_MODEL
    ev = np.arange(0, HEAD_DIM, 2)
    od = np.arange(1, HEAD_DIM, 2)
    kvw = GQA_KV_HEADS * HEAD_DIM
    naw = NA_HEADS * HEAD_DIM
    zero_col = prm["w_in"].shape[1]
    pad64 = np.full(LANES - HEAD_DIM, zero_col)
    q0 = 2 * kvw + 2 * naw
    cols = np.concatenate(
        [np.concatenate([g * HEAD_DIM + ev, g * HEAD_DIM + od]) for g in range(GQA_KV_HEADS)]
        + [np.concatenate([kvw + g * HEAD_DIM + np.arange(HEAD_DIM), pad64]) for g in range(GQA_KV_HEADS)]
        + [2 * kvw + np.arange(naw)]
        + [np.concatenate([2 * kvw + naw + h * HEAD_DIM + np.arange(HEAD_DIM), pad64]) for h in range(NA_HEADS)]
        + [np.concatenate([q0 + h * HEAD_DIM + ev, q0 + h * HEAD_DIM + od]) for h in GQA_ORDER]
        + [q0 + GQA_HEADS * HEAD_DIM + np.arange(naw)])
    assert cols.shape[0] == O_COLS
    w_ext = jnp.concatenate([prm["w_in"], jnp.zeros((d, 1), F32)], axis=1)
    win = jnp.take(w_ext, cols, axis=1).astype(BF16)

    def row(v):
        return jnp.pad(v.astype(F32), (0, HEADS_W - v.shape[0]))

    perm = np.concatenate([ev, od])
    q_scale = (HEAD_DIM ** -0.5) * LOG2E
    ones_lane = np.tile((np.arange(LANES) == HEAD_DIM).astype(np.float32), NA_HEADS)
    vec = jnp.stack([
        row(prm["mix_norm"]),
        row(jnp.tile(prm["gqa_k_gain"][perm], GQA_KV_HEADS)),
        row(jnp.tile(prm["na_k_gain"], NA_HEADS)),
        row(jnp.tile(prm["gqa_q_gain"][perm], GQA_HEADS) * q_scale),
        row(jnp.tile(prm["na_q_gain"], NA_HEADS) * q_scale),
        jnp.asarray(ones_lane),
    ] + [jnp.zeros((HEADS_W,), F32)] * 2)
    seg_id = np.arange(MXU_DIM) // HEAD_DIM
    gmat = jnp.asarray((seg_id[:, None] == seg_id[None, :]).astype(np.float32)).astype(BF16)
    return win, vec, gmat


def _gqa_rope_tables(n_tokens, use_rope):
    if not use_rope:
        c = jnp.ones((n_tokens, LANES), F32)
        return c, jnp.zeros_like(c)
    cos, sin = _axial_angles(n_tokens, HEAD_DIM)
    return jnp.concatenate([cos, cos, cos, cos], axis=-1), jnp.concatenate([-sin, sin, -sin, sin], axis=-1)


def _odd_proj(x, mod2, weights, use_rope):
    win, vec, gmat = weights
    b, l, d = x.shape
    tm = min(TOKEN_TILE, l)
    cos, sin = _gqa_rope_tables(l, use_rope)
    tok = lambda w: pl.BlockSpec((1, tm, w), lambda i, j: (i, j, 0))
    widths = (GQA_HEADS * HEAD_DIM, GQA_KV_HEADS * HEAD_DIM, GQA_KV_HEADS * LANES,
              NA_HEADS * HEAD_DIM, NA_HEADS * HEAD_DIM, NA_HEADS * LANES)
    return pl.pallas_call(
        _odd_proj_kernel,
        grid=(b, l // tm),
        in_specs=[tok(d),
                  pl.BlockSpec((1, 2, d), lambda i, j: (i, 0, 0)),
                  _resident(vec.shape), _resident(win.shape), _resident(gmat.shape),
                  pl.BlockSpec((tm, LANES), lambda i, j: (j, 0)),
                  pl.BlockSpec((tm, LANES), lambda i, j: (j, 0))],
        out_specs=[tok(w) for w in widths],
        out_shape=[jax.ShapeDtypeStruct((b, l, w), BF16) for w in widths],
        compiler_params=_params("parallel", "parallel"),
    )(x, mod2, vec, win, gmat, cos, sin)


def _pair_queries(q, packed):
    if not packed:
        return q[:, 0:LANES], q[:, LANES:2 * LANES]
    lane = lax.broadcasted_iota(jnp.int32, q.shape, 1)
    zero = jnp.zeros_like(q)
    return jnp.where(lane < HEAD_DIM, q, zero), jnp.where(lane >= HEAD_DIM, q, zero)


def _pair_output(acc_a, acc_b):
    oa = acc_a / acc_a[:, HEAD_DIM:HEAD_DIM + 1]
    ob = acc_b / acc_b[:, HEAD_DIM:HEAD_DIM + 1]
    lane = lax.broadcasted_iota(jnp.int32, oa.shape, 1)
    return jnp.where(lane < HEAD_DIM, oa, pltpu.roll(ob, HEAD_DIM, 1))


def _score(q, k):
    return lax.dot_general(q, k, (((1,), (1,)), ((), ())), preferred_element_type=F32)


def _attn_kernel(*refs, n_src, packed, src_len):
    q_ref = refs[0]
    kv_refs = refs[1:1 + 2 * n_src]
    o_ref = refs[1 + 2 * n_src]
    qs = _pair_queries(q_ref[0], packed)
    rows = qs[0].shape[0]
    m = [jnp.full((rows, 1), NEG_BIG, F32) for _ in range(2)]
    acc = [jnp.zeros((rows, LANES), F32) for _ in range(2)]
    for s in range(n_src):
        k_ref, v_ref = kv_refs[2 * s], kv_refs[2 * s + 1]
        tk = min(ATT_TK, src_len[s])
        for blk in range(src_len[s] // tk):
            rs = slice(blk * tk, (blk + 1) * tk)
            for hd in range(2):
                ks = slice(0, LANES) if packed else slice(hd * LANES, (hd + 1) * LANES)
                sc = _score(qs[hd], k_ref[0, rs, ks])
                m_new = jnp.maximum(m[hd], jnp.max(sc, axis=-1, keepdims=True))
                p = jnp.exp2(sc - m_new).astype(BF16)
                pv = jnp.dot(p, v_ref[0, rs, hd * LANES:(hd + 1) * LANES], preferred_element_type=F32)
                acc[hd] = acc[hd] * jnp.exp2(m[hd] - m_new) + pv
                m[hd] = m_new
    o_ref[0] = _pair_output(acc[0], acc[1]).astype(o_ref.dtype)


def _pair_attention(q, sources, packed):
    b, lq, qw = q.shape
    wq = LANES if packed else 2 * LANES
    n_pairs = qw // wq
    tq = min(ATT_TQ, lq)
    in_specs = [pl.BlockSpec((1, tq, wq), lambda i, p, j: (i, j, p))]
    args = [q]
    for k, v in sources:
        lk = k.shape[1]
        if packed:
            in_specs.append(pl.BlockSpec((1, lk, LANES), lambda i, p, j: (i, 0, 0)))
            in_specs.append(pl.BlockSpec((1, lk, 2 * LANES), lambda i, p, j: (i, 0, 0)))
        else:
            in_specs.append(pl.BlockSpec((1, lk, 2 * LANES), lambda i, p, j: (i, 0, p)))
            in_specs.append(pl.BlockSpec((1, lk, 2 * LANES), lambda i, p, j: (i, 0, p)))
        args += [k, v]
    kern = functools.partial(_attn_kernel, n_src=len(sources), packed=packed,
                             src_len=tuple(k.shape[1] for k, _ in sources))
    return pl.pallas_call(
        kern,
        grid=(b, n_pairs, lq // tq),
        in_specs=in_specs,
        out_specs=pl.BlockSpec((1, tq, LANES), lambda i, p, j: (i, j, p)),
        out_shape=jax.ShapeDtypeStruct((b, lq, n_pairs * LANES), BF16),
        compiler_params=_params("parallel", "parallel", "arbitrary"),
    )(*args)


def _na_ctx_kernel(q_ref, k_ref, v_ref, o_ref):
    qs = _pair_queries(q_ref[0], True)
    accs = []
    for hd in range(2):
        sc = _score(qs[hd], k_ref[0])
        p = jnp.exp2(sc - jnp.max(sc, axis=-1, keepdims=True)).astype(BF16)
        accs.append(jnp.dot(p, v_ref[0, :, hd * LANES:(hd + 1) * LANES], preferred_element_type=F32))
    o_ref[0] = _pair_output(accs[0], accs[1]).astype(o_ref.dtype)


def _na_ctx_attention(q, k, v):
    b, l, w = q.shape
    n_pairs = w // LANES
    return pl.pallas_call(
        _na_ctx_kernel,
        grid=(b, n_pairs),
        in_specs=[pl.BlockSpec((1, l, LANES), lambda i, p: (i, 0, p)),
                  pl.BlockSpec((1, l, LANES), lambda i, p: (i, 0, p)),
                  pl.BlockSpec((1, l, 2 * LANES), lambda i, p: (i, 0, p))],
        out_specs=pl.BlockSpec((1, l, LANES), lambda i, p: (i, 0, p)),
        out_shape=jax.ShapeDtypeStruct((b, l, w), BF16),
        compiler_params=_params("parallel", "parallel"),
    )(q, k, v)


def _na_kernel(q_ref, k_ref, v_ref, kc_ref, vc_ref, bias_ref, o_ref, *, n_blocks, grid_rows):
    blk_q = NA_ROWS * GRID_W
    win = NA_WIN_ROWS * GRID_W
    kc = kc_ref[0]

    def body(i, carry):
        q0 = pl.multiple_of(i * blk_q, blk_q)
        ws = jnp.clip(i * NA_ROWS - WIN_H // 2, 0, grid_rows - NA_WIN_ROWS)
        k0 = pl.multiple_of(ws * GRID_W, GRID_W)
        cfg = jnp.where(i == 0, 0, jnp.where(i == n_blocks - 1, 2, 1))
        qs = _pair_queries(q_ref[0, pl.ds(q0, blk_q), :], True)
        kw = k_ref[0, pl.ds(k0, win), :]
        accs = []
        for hd in range(2):
            s_nb = _score(qs[hd], kw) + bias_ref[0, cfg, hd]
            s_c = _score(qs[hd], kc)
            m = jnp.maximum(jnp.max(s_nb, axis=-1, keepdims=True), jnp.max(s_c, axis=-1, keepdims=True))
            p_nb = jnp.exp2(s_nb - m).astype(BF16)
            p_c = jnp.exp2(s_c - m).astype(BF16)
            vs = slice(hd * LANES, (hd + 1) * LANES)
            accs.append(jnp.dot(p_nb, v_ref[0, pl.ds(k0, win), vs], preferred_element_type=F32)
                        + jnp.dot(p_c, vc_ref[0, :, vs], preferred_element_type=F32))
        o_ref[0, pl.ds(q0, blk_q), :] = _pair_output(accs[0], accs[1]).astype(o_ref.dtype)
        return carry

    lax.fori_loop(0, n_blocks, body, 0, unroll=NA_UNROLL)


def _na_bias_table(rpb, grid_rows):
    n_blocks = grid_rows // NA_ROWS
    assert grid_rows % NA_ROWS == 0 and grid_rows >= NA_WIN_ROWS + NA_ROWS
    n_dr, n_dc = 2 * WIN_H - 1, 2 * WIN_W - 1
    ri = np.arange(NA_ROWS)[:, None]
    kj = np.arange(NA_WIN_ROWS)[None, :]
    row_sel, row_valid = [], []
    for blk in (0, 1, n_blocks - 1):
        ws = int(np.clip(blk * NA_ROWS - WIN_H // 2, 0, grid_rows - NA_WIN_ROWS))
        r = blk * NA_ROWS + ri
        krow = ws + kj
        rs = np.clip(r - WIN_H // 2, 0, grid_rows - WIN_H)
        row_valid.append((krow >= rs) & (krow < rs + WIN_H))
        row_sel.append(np.eye(n_dr, dtype=np.float32)[np.clip(krow - r + (WIN_H - 1), 0, n_dr - 1)])
    row_sel = np.stack(row_sel)
    row_valid = np.stack(row_valid)
    cq = np.arange(GRID_W)[:, None]
    ck = np.arange(GRID_W)[None, :]
    cs = np.clip(cq - WIN_W // 2, 0, GRID_W - WIN_W)
    col_valid = (ck >= cs) & (ck < cs + WIN_W)
    col_sel = np.eye(n_dc, dtype=np.float32)[np.clip(ck - cq + (WIN_W - 1), 0, n_dc - 1)]
    t = jnp.einsum("cijd,hde->hcije", row_sel, rpb, precision=lax.Precision.HIGHEST)
    t = jnp.einsum("hcije,qke->hciqjk", t, col_sel, precision=lax.Precision.HIGHEST)
    valid = row_valid[:, :, None, :, None] & col_valid[None, None, :, None, :]
    tab = jnp.where(jnp.asarray(valid)[None], t * LOG2E, NEG_BIG)
    h = tab.shape[0]
    tab = tab.reshape(h // 2, 2, 3, NA_ROWS * GRID_W, NA_WIN_ROWS * GRID_W)
    return tab.transpose(0, 2, 1, 3, 4)


def _neighbourhood_attention(q, k, v, kc, vc, bias):
    b, s, w = q.shape
    n_pairs = w // LANES
    grid_rows = s // GRID_W
    n_blocks = grid_rows // NA_ROWS
    lc = kc.shape[1]
    return pl.pallas_call(
        functools.partial(_na_kernel, n_blocks=n_blocks, grid_rows=grid_rows),
        grid=(b, n_pairs),
        in_specs=[pl.BlockSpec((1, s, LANES), lambda i, p: (i, 0, p)),
                  pl.BlockSpec((1, s, LANES), lambda i, p: (i, 0, p)),
                  pl.BlockSpec((1, s, 2 * LANES), lambda i, p: (i, 0, p)),
                  pl.BlockSpec((1, lc, LANES), lambda i, p: (i, 0, p)),
                  pl.BlockSpec((1, lc, 2 * LANES), lambda i, p: (i, 0, p)),
                  pl.BlockSpec((1,) + bias.shape[1:], lambda i, p: (p, 0, 0, 0, 0))],
        out_specs=pl.BlockSpec((1, s, LANES), lambda i, p: (i, 0, p)),
        out_shape=jax.ShapeDtypeStruct((b, s, w), BF16),
        compiler_params=_params("parallel", "arbitrary"),
    )(q, k, v, kc, vc, bias)


def _conv_kernel(y_ref, w_ref, vec_ref, o_ref, pad_ref, shift_ref, *, seq):
    ch = y_ref.shape[-1]
    zeros = jnp.zeros((CONV_HALO, ch), F32)
    pad_ref[0:CONV_HALO, :] = zeros
    pad_ref[CONV_HALO + seq:2 * CONV_HALO + seq, :] = zeros
    pad_ref[CONV_HALO:CONV_HALO + seq, :] = y_ref[0]
    first = CONV_HALO - CONV_WIDTH // 2

    span = CONV_TILE + 2 * CONV_HALO - SUBLANES

    def body(i, carry):
        base = pl.multiple_of(i * CONV_TILE, CONV_TILE)
        window = pad_ref[pl.ds(base, CONV_TILE + 2 * CONV_HALO), :]
        for r in range(1, SUBLANES):
            shift_ref[r - 1] = window[r:r + span, :]
        acc = jnp.zeros((CONV_TILE, ch), F32)
        for r in range(SUBLANES):
            for a in range((2 * CONV_HALO) // SUBLANES):
                t = SUBLANES * a + r - first
                if 0 <= t < CONV_WIDTH:
                    rows = slice(SUBLANES * a, SUBLANES * a + CONV_TILE)
                    tap = window[rows, :] if r == 0 else shift_ref[r - 1, rows, :]
                    acc = acc + tap * w_ref[t:t + 1, :]
        acc = acc + vec_ref[0:1, :]
        mu = jnp.mean(acc, axis=-1, keepdims=True)
        cen = acc - mu
        var = jnp.mean(cen * cen, axis=-1, keepdims=True)
        z = cen * lax.rsqrt(var + EPS) * vec_ref[1:2, :] + vec_ref[2:3, :]
        o_ref[0, pl.ds(base, CONV_TILE), :] = _silu(z).astype(o_ref.dtype)
        return carry

    lax.fori_loop(0, seq // CONV_TILE, body, 0)


def _conv_module(y, prm):
    b, l, ch = y.shape
    vec = jnp.stack([prm["conv_dw_b"], prm["conv_ln_g"], prm["conv_ln_b"]] + [jnp.zeros((ch,), F32)] * 5)
    w = jnp.pad(prm["conv_dw_w"], ((0, 1), (0, 0)))
    return pl.pallas_call(
        functools.partial(_conv_kernel, seq=l),
        grid=(b,),
        in_specs=[pl.BlockSpec((1, l, ch), lambda i: (i, 0, 0)), _resident(w.shape), _resident(vec.shape)],
        out_specs=pl.BlockSpec((1, l, ch), lambda i: (i, 0, 0)),
        out_shape=jax.ShapeDtypeStruct((b, l, ch), BF16),
        scratch_shapes=[pltpu.VMEM((l + 2 * CONV_HALO, ch), F32),
                        pltpu.VMEM((SUBLANES - 1, CONV_TILE + 2 * CONV_HALO - SUBLANES, ch), F32)],
        compiler_params=_params("parallel"),
    )(y, w, vec)


def _out_kernel(x_ref, a_ref, b_ref, gate_ref, wa_ref, wb_ref, o_ref):
    y = (jnp.dot(a_ref[0], wa_ref[...], preferred_element_type=F32)
         + jnp.dot(b_ref[0], wb_ref[...], preferred_element_type=F32))
    o_ref[0] = x_ref[0] + gate_ref[0] * y


def _mixer_out(x, part_a, part_b, gate, w_a, w_b):
    b, l, d = x.shape
    tm = min(TOKEN_TILE, l)
    ha = part_a.shape[-1]
    hb = part_b.shape[-1]
    tok = lambda w: pl.BlockSpec((1, tm, w), lambda i, j: (i, j, 0))
    return pl.pallas_call(
        _out_kernel,
        grid=(b, l // tm),
        in_specs=[tok(d), tok(ha), tok(hb), pl.BlockSpec((1, 1, d), lambda i, j: (i, 0, 0)),
                  _resident((ha, d)), _resident((hb, d))],
        out_specs=tok(d),
        out_shape=jax.ShapeDtypeStruct((b, l, d), F32),
        compiler_params=_params("parallel", "parallel"),
    )(x, part_a, part_b, gate, w_a.astype(BF16), w_b.astype(BF16))


def _flat_ctx(a, batch):
    if a.shape[0] == batch:
        return a.reshape(1, batch * a.shape[1], a.shape[2])
    return a.reshape(batch, a.shape[1] // batch, a.shape[2])


def _trunk_layer(xl, xc, mods, prm, even, ctx_out):
    batch = xl.shape[0]
    mod_l, mod_c = mods
    xl = _half_ffn(xl, mod_l[:, 0:3], prm["ffn1_norm"], prm["ffn1_w_in"], prm["ffn1_w_out"])
    xc = _half_ffn(xc, mod_c[:, 0:3], prm["ffn1_norm"], prm["ffn1_w_in"], prm["ffn1_w_out"])
    w_out = prm["w_out"]
    half = w_out.shape[0] // 2
    if even:
        weights = _even_weights(prm)
        ql, kl, vl, yl = _even_proj(xl, mod_l[:, 3:5], weights, True)
        qc, kc, vc, yc = _even_proj(xc, mod_c[:, 3:5], weights, False)
        qc, kc, vc, yc = (_flat_ctx(a, batch) for a in (qc, kc, vc, yc))
        att_l = _pair_attention(ql, [(kc, vc), (kl, vl)], packed=False)
        conv_l = _conv_module(yl, prm)
        w_a, w_b = w_out[:half], w_out[half:]
        xl = _mixer_out(xl, att_l, conv_l, mod_l[:, 5:6], w_a, w_b)
        if ctx_out:
            att_c = _pair_attention(qc, [(kc, vc)], packed=False)
            conv_c = _conv_module(yc, prm)
            xc = _mixer_out(xc, _flat_ctx(att_c, batch), _flat_ctx(conv_c, batch), mod_c[:, 5:6], w_a, w_b)
    else:
        weights = _odd_weights(prm)
        cql, ckl, cvl, nql, nkl, nvl = _odd_proj(xl, mod_l[:, 3:5], weights, True)
        ctx_parts = [_flat_ctx(a, batch) for a in _odd_proj(xc, mod_c[:, 3:5], weights, False)]
        cqc, ckc, cvc, nqc, nkc, nvc = ctx_parts
        gqa_l = _pair_attention(cql, [(ckc, cvc), (ckl, cvl)], packed=True)
        bias = _na_bias_table(prm["na_rpb"], xl.shape[1] // GRID_W)
        na_l = _neighbourhood_attention(nql, nkl, nvl, nkc, nvc, bias)
        order = np.concatenate([h * HEAD_DIM + np.arange(HEAD_DIM) for h in GQA_ORDER])
        w_a, w_b = w_out[:half][order], w_out[half:]
        xl = _mixer_out(xl, gqa_l, na_l, mod_l[:, 5:6], w_a, w_b)
        if ctx_out:
            gqa_c = _pair_attention(cqc, [(ckc, cvc)], packed=True)
            na_c = _na_ctx_attention(nqc, nkc, nvc)
            xc = _mixer_out(xc, _flat_ctx(gqa_c, batch), _flat_ctx(na_c, batch), mod_c[:, 5:6], w_a, w_b)
    xl = _half_ffn(xl, mod_l[:, 6:9], prm["ffn2_norm"], prm["ffn2_w_in"], prm["ffn2_w_out"])
    if ctx_out:
        xc = _half_ffn(xc, mod_c[:, 6:9], prm["ffn2_norm"], prm["ffn2_w_in"], prm["ffn2_w_out"])
    return xl, xc


def _layer_mods(c, c_ctx, prm):
    batch, d = c.shape
    rows = -(-(batch + 1) // 8) * 8
    cond = jnp.zeros((rows, d), F32).at[:batch].set(c).at[batch].set(c_ctx)
    mod = _modulation(cond, prm["mod_w"], prm["mod_b"]).reshape(rows, N_MOD, d)
    return mod[:batch], mod[batch:batch + 1]


def kernel(x, c, ctx, c_ctx,
           l0_mod_w, l0_mod_b, l0_ffn1_norm, l0_ffn1_w_in, l0_ffn1_w_out, l0_mix_norm, l0_w_in,
           l0_mla_q_norm, l0_mla_w_uq, l0_mla_kv_norm, l0_mla_w_ukv, l0_mla_q_gain, l0_mla_k_gain,
           l0_conv_glu_b, l0_conv_dw_w, l0_conv_dw_b, l0_conv_ln_g, l0_conv_ln_b,
           l0_w_out, l0_ffn2_norm, l0_ffn2_w_in, l0_ffn2_w_out,
           l1_mod_w, l1_mod_b, l1_ffn1_norm, l1_ffn1_w_in, l1_ffn1_w_out, l1_mix_norm, l1_w_in,
           l1_gqa_q_gain, l1_gqa_k_gain, l1_na_q_gain, l1_na_k_gain, l1_na_rpb,
           l1_w_out, l1_ffn2_norm, l1_ffn2_w_in, l1_ffn2_w_out):
    layers = (
        dict(mod_w=l0_mod_w, mod_b=l0_mod_b, ffn1_norm=l0_ffn1_norm, ffn1_w_in=l0_ffn1_w_in,
             ffn1_w_out=l0_ffn1_w_out, mix_norm=l0_mix_norm, w_in=l0_w_in,
             mla_q_norm=l0_mla_q_norm, mla_w_uq=l0_mla_w_uq, mla_kv_norm=l0_mla_kv_norm,
             mla_w_ukv=l0_mla_w_ukv, mla_q_gain=l0_mla_q_gain, mla_k_gain=l0_mla_k_gain,
             conv_glu_b=l0_conv_glu_b, conv_dw_w=l0_conv_dw_w, conv_dw_b=l0_conv_dw_b,
             conv_ln_g=l0_conv_ln_g, conv_ln_b=l0_conv_ln_b, w_out=l0_w_out,
             ffn2_norm=l0_ffn2_norm, ffn2_w_in=l0_ffn2_w_in, ffn2_w_out=l0_ffn2_w_out),
        dict(mod_w=l1_mod_w, mod_b=l1_mod_b, ffn1_norm=l1_ffn1_norm, ffn1_w_in=l1_ffn1_w_in,
             ffn1_w_out=l1_ffn1_w_out, mix_norm=l1_mix_norm, w_in=l1_w_in,
             gqa_q_gain=l1_gqa_q_gain, gqa_k_gain=l1_gqa_k_gain, na_q_gain=l1_na_q_gain,
             na_k_gain=l1_na_k_gain, na_rpb=l1_na_rpb, w_out=l1_w_out,
             ffn2_norm=l1_ffn2_norm, ffn2_w_in=l1_ffn2_w_in, ffn2_w_out=l1_ffn2_w_out),
    )
    batch = x.shape[0]
    xl, xc = x, _flat_ctx(ctx, batch)
    for i, prm in enumerate(layers):
        mods = _layer_mods(c, c_ctx, prm)
        xl, xc = _trunk_layer(xl, xc, mods, prm, even=(i % 2 == 0), ctx_out=(i < len(layers) - 1))
    return xl
```

```python
import functools
import math

import numpy as np
import jax
import jax.numpy as jnp
from jax import lax
from jax.experimental import pallas as pl
from jax.experimental.pallas import tpu as pltpu

F32 = jnp.float32
BF16 = jnp.bfloat16

D_MODEL = 1024
CTX_LEN = 256
GRID_W = 64
D_FF = 2816
N_MOD = 9
EPS = 1e-6
ROPE_BASE = 10000.0
MLA_HEADS = 8
MLA_NOPE = 64
MLA_ROPE = 32
MLA_V = 64
MLA_Q_LORA = 384
MLA_KV_LORA = 256
CONV_CH = 512
CONV_WIDTH = 31
HEAD_DIM = 64
GQA_HEADS = 8
GQA_KV_HEADS = 2
NA_HEADS = 8
WIN_H = 8
WIN_W = 16

LANES = 128
SUBLANES = 8
MXU_DIM = 256
VMEM_LIMIT = 56 * 1024 * 1024
LOG2E = math.log2(math.e)
NEG_BIG = -1e30

FFN_CHUNK = 256
TOKEN_TILE = 512
FFN_TILE = 1024
ATT_TQ = 512
ATT_TK = 256
NA_ROWS = 4
NA_WIN_ROWS = 12
NA_UNROLL = 4
CONV_TILE = 128
CONV_HALO = 16


def _params(*sem):
    return pltpu.CompilerParams(dimension_semantics=sem, vmem_limit_bytes=VMEM_LIMIT)


def _resident(shape):
    return pl.BlockSpec(shape, lambda *_: (0,) * len(shape), pipeline_mode=pl.Buffered(1))


def _silu(x):
    return x * jax.nn.sigmoid(x)


def _rms_rows(x):
    return x * lax.rsqrt(jnp.mean(x * x, axis=-1, keepdims=True) + EPS)


def _seg_sumsq(x, g):
    outs = []
    for p in range(x.shape[-1] // MXU_DIM):
        sq = x[:, p * MXU_DIM:(p + 1) * MXU_DIM]
        outs.append(jnp.dot((sq * sq).astype(BF16), g, preferred_element_type=F32))
    return outs[0] if len(outs) == 1 else jnp.concatenate(outs, axis=-1)


def _swap_halves(x, half, lo):
    lane = lax.broadcasted_iota(jnp.int32, x.shape, 1)
    first = ((lane - lo) & (2 * half - 1)) < half
    return jnp.where(first, pltpu.roll(x, LANES - half, 1), pltpu.roll(x, half, 1))


def _mod_kernel(c_ref, w_ref, b_ref, o_ref):
    a = _silu(c_ref[...]).astype(BF16)
    o_ref[...] = jnp.dot(a, w_ref[...].astype(BF16), preferred_element_type=F32) + b_ref[...]


def _modulation(cond, mod_w, mod_b):
    rows, d = cond.shape
    n = mod_w.shape[1]
    return pl.pallas_call(
        _mod_kernel,
        grid=(n // d,),
        in_specs=[pl.BlockSpec((rows, d), lambda j: (0, 0)),
                  pl.BlockSpec((d, d), lambda j: (0, j)),
                  pl.BlockSpec((1, d), lambda j: (0, j))],
        out_specs=pl.BlockSpec((rows, d), lambda j: (0, j)),
        out_shape=jax.ShapeDtypeStruct((rows, n), F32),
        compiler_params=_params("arbitrary"),
    )(cond, mod_w, mod_b.reshape(1, n))


def _ffn_kernel(*refs, mixer):
    if mixer:
        x_ref, mod_ref, g_ref, win_ref, wout_ref, pa_ref, pb_ref, wa_ref, wb_ref, o_ref, h_ref, a_ref = refs
    else:
        x_ref, mod_ref, g_ref, win_ref, wout_ref, o_ref, h_ref, a_ref = refs
    x = x_ref[0]
    mod = mod_ref[0]
    if mixer:
        mix = (jnp.dot(pa_ref[0], wa_ref[...], preferred_element_type=F32)
               + jnp.dot(pb_ref[0], wb_ref[...], preferred_element_type=F32))
        x = x + mod[3:4] * mix
    h = _rms_rows(x) * g_ref[...]
    h_ref[...] = (h * (1.0 + mod[1:2]) + mod[0:1]).astype(BF16)
    for c in range(D_FF // FFN_CHUNK):
        gate = jnp.dot(h_ref[...], win_ref[:, c * FFN_CHUNK:(c + 1) * FFN_CHUNK], preferred_element_type=F32)
        up = jnp.dot(h_ref[...], win_ref[:, D_FF + c * FFN_CHUNK:D_FF + (c + 1) * FFN_CHUNK], preferred_element_type=F32)
        a_ref[:, c * FFN_CHUNK:(c + 1) * FFN_CHUNK] = (_silu(gate) * up).astype(BF16)
    y = jnp.dot(a_ref[...], wout_ref[...], preferred_element_type=F32)
    o_ref[0] = x + (0.5 * mod[2:3]) * y


def _half_ffn(x, mod, norm_g, w_in, w_out, mixer=None):
    b, l, d = x.shape
    tm = min(FFN_TILE, l)
    tok = lambda w: pl.BlockSpec((1, tm, w), lambda i, j: (i, j, 0))
    in_specs = [tok(d), pl.BlockSpec((1,) + mod.shape[1:], lambda i, j: (i, 0, 0)),
                _resident((1, d)), _resident(w_in.shape), _resident(w_out.shape)]
    args = [x, mod, norm_g.reshape(1, d), w_in.astype(BF16), w_out.astype(BF16)]
    if mixer is not None:
        part_a, part_b, w_a, w_b = mixer
        in_specs += [tok(part_a.shape[-1]), tok(part_b.shape[-1]), _resident(w_a.shape), _resident(w_b.shape)]
        args += [part_a, part_b, w_a.astype(BF16), w_b.astype(BF16)]
    return pl.pallas_call(
        functools.partial(_ffn_kernel, mixer=mixer is not None),
        grid=(b, l // tm),
        in_specs=in_specs,
        out_specs=tok(d),
        out_shape=jax.ShapeDtypeStruct((b, l, d), F32),
        scratch_shapes=[pltpu.VMEM((tm, d), BF16), pltpu.VMEM((tm, D_FF), BF16)],
        compiler_params=_params("parallel", "parallel"),
    )(*args)


E_KR = MLA_KV_LORA
E_CQ = E_KR + LANES
E_CA = E_CQ + MLA_Q_LORA
E_CG = E_CA + CONV_CH
E_COLS = E_CG + CONV_CH
HEADS_W = MLA_HEADS * LANES


def _even_proj_kernel(x_ref, mod_ref, vec_ref, win_ref, wuk_ref, wuv_ref, wuq_ref, g_ref, c_ref, s_ref,
                      q_ref, k_ref, v_ref, y_ref):
    x = x_ref[0]
    mod = mod_ref[0]
    h = _rms_rows(x) * vec_ref[0:1, :]
    h = (h * (1.0 + mod[1:2]) + mod[0:1]).astype(BF16)
    pp = jnp.dot(h, win_ref[...], preferred_element_type=F32)
    cos = c_ref[...]
    sin = s_ref[...]
    g = g_ref[...]

    ckv = (_rms_rows(pp[:, 0:E_KR]) * vec_ref[1:2, 0:MLA_KV_LORA]).astype(BF16)
    kn = jnp.dot(ckv, wuk_ref[...], preferred_element_type=F32)
    vv = jnp.dot(ckv, wuv_ref[...], preferred_element_type=F32)
    kn = kn * lax.rsqrt(_seg_sumsq(kn, g) * vec_ref[6:7, :] + EPS) * vec_ref[3:4, :]
    kr = pp[:, E_KR:E_CQ]
    kr = kr * lax.rsqrt(jnp.sum(kr * kr, axis=-1, keepdims=True) * (1.0 / MLA_ROPE) + EPS) * vec_ref[4:5, 0:LANES]
    kr = kr * cos + _swap_halves(kr, MLA_ROPE // 2, MLA_NOPE) * sin
    v_ref[0] = (vv + vec_ref[8:9, :]).astype(BF16)

    cq = (_rms_rows(pp[:, E_CQ:E_CA]) * vec_ref[2:3, 0:MLA_Q_LORA]).astype(BF16)
    q = jnp.dot(cq, wuq_ref[...], preferred_element_type=F32)
    q = q * lax.rsqrt(_seg_sumsq(q, g) * vec_ref[6:7, :] + EPS) * vec_ref[5:6, :]
    for hd in range(MLA_HEADS):
        sl = slice(hd * LANES, (hd + 1) * LANES)
        k_ref[0, :, sl] = (kn[:, sl] + kr).astype(BF16)
        qh = q[:, sl]
        qh = qh * cos + _swap_halves(qh, MLA_ROPE // 2, MLA_NOPE) * sin
        q_ref[0, :, sl] = qh.astype(BF16)

    glu_b = vec_ref[7:8, :]
    y_ref[0] = (pp[:, E_CA:E_CG] + glu_b[:, 0:CONV_CH]) * jax.nn.sigmoid(pp[:, E_CG:E_COLS] + glu_b[:, CONV_CH:])


def _even_weights(prm):
    d = D_MODEL
    ev = np.arange(0, MLA_ROPE, 2)
    od = np.arange(1, MLA_ROPE, 2)
    zero_col = prm["w_in"].shape[1]
    cols = np.concatenate([
        np.arange(MLA_KV_LORA),
        np.full(MLA_NOPE, zero_col), MLA_KV_LORA + ev, MLA_KV_LORA + od, np.full(LANES - MLA_NOPE - MLA_ROPE, zero_col),
        np.arange(MLA_KV_LORA + MLA_ROPE, zero_col)])
    w_ext = jnp.concatenate([prm["w_in"], jnp.zeros((d, 1), F32)], axis=1)
    win = jnp.take(w_ext, cols, axis=1).astype(BF16)

    qd = MLA_NOPE + MLA_ROPE
    zq = MLA_HEADS * qd
    qcols = np.concatenate([np.concatenate([h * qd + np.arange(MLA_NOPE), h * qd + MLA_NOPE + ev, h * qd + MLA_NOPE + od,
                                            np.full(LANES - qd, zq)]) for h in range(MLA_HEADS)])
    wuq = jnp.take(jnp.concatenate([prm["mla_w_uq"], jnp.zeros((MLA_Q_LORA, 1), F32)], axis=1), qcols, axis=1).astype(BF16)
    kvd = MLA_NOPE + MLA_V
    zk = MLA_HEADS * kvd
    kcols = np.concatenate([np.concatenate([h * kvd + np.arange(MLA_NOPE), np.full(LANES - MLA_NOPE, zk)])
                            for h in range(MLA_HEADS)])
    vcols = np.concatenate([np.concatenate([h * kvd + MLA_NOPE + np.arange(MLA_V), np.full(LANES - MLA_V, zk)])
                            for h in range(MLA_HEADS)])
    wukv = jnp.concatenate([prm["mla_w_ukv"], jnp.zeros((MLA_KV_LORA, 1), F32)], axis=1)
    wuk = jnp.take(wukv, kcols, axis=1).astype(BF16)
    wuv = jnp.take(wukv, vcols, axis=1).astype(BF16)

    def row(v):
        return jnp.pad(v.astype(F32), (0, HEADS_W - v.shape[0]))

    rope_perm = np.concatenate([MLA_NOPE + ev, MLA_NOPE + od])
    pad_q = jnp.zeros((LANES - qd,), F32)
    kg, qg = prm["mla_k_gain"], prm["mla_q_gain"]
    q_scale = (qd ** -0.5) * LOG2E
    inv_cnt = np.tile(np.concatenate([np.full(MLA_NOPE, 1.0 / MLA_NOPE), np.full(MLA_ROPE, 1.0 / MLA_ROPE),
                                      np.ones(LANES - qd)]), MLA_HEADS).astype(np.float32)
    ones_lane = np.tile((np.arange(LANES) == MLA_V).astype(np.float32), MLA_HEADS)
    vec = jnp.stack([
        row(prm["mix_norm"]),
        row(prm["mla_kv_norm"]),
        row(prm["mla_q_norm"]),
        jnp.tile(jnp.concatenate([kg[:MLA_NOPE], jnp.zeros((LANES - MLA_NOPE,), F32)]), MLA_HEADS),
        row(jnp.concatenate([jnp.zeros((MLA_NOPE,), F32), kg[rope_perm], pad_q])),
        jnp.tile(jnp.concatenate([qg[:MLA_NOPE], qg[rope_perm], pad_q]), MLA_HEADS) * q_scale,
        jnp.asarray(inv_cnt),
        row(prm["conv_glu_b"]),
        jnp.asarray(ones_lane),
    ] + [jnp.zeros((HEADS_W,), F32)] * 7)

    seg = np.arange(MXU_DIM)
    seg_id = (seg // LANES) * 4 + np.where(seg % LANES < MLA_NOPE, 0, np.where(seg % LANES < qd, 1, 2))
    gmat = jnp.asarray((seg_id[:, None] == seg_id[None, :]).astype(np.float32)).astype(BF16)
    return win, wuk, wuv, wuq, vec, gmat


def _mla_rope_tables(n_tokens, use_rope):
    ones = jnp.ones((n_tokens, MLA_NOPE), F32)
    tail = jnp.ones((n_tokens, LANES - MLA_NOPE - MLA_ROPE), F32)
    if not use_rope:
        c = jnp.ones((n_tokens, LANES), F32)
        return c, jnp.zeros_like(c)
    cos, sin = _axial_angles(n_tokens, MLA_ROPE)
    c = jnp.concatenate([ones, cos, cos, tail], axis=-1)
    s = jnp.concatenate([0 * ones, -sin, sin, 0 * tail], axis=-1)
    return c, s


def _axial_angles(n_tokens, dim):
    t = jnp.arange(n_tokens)
    row = (t // GRID_W).astype(F32)
    col = (t % GRID_W).astype(F32)
    n_pairs = dim // 4
    inv = ROPE_BASE ** (-jnp.arange(n_pairs, dtype=F32) / n_pairs)
    ang = jnp.concatenate([row[:, None] * inv, col[:, None] * inv], axis=-1)
    return jnp.cos(ang), jnp.sin(ang)


def _even_proj(x, mod2, weights, use_rope):
    win, wuk, wuv, wuq, vec, gmat = weights
    b, l, d = x.shape
    tm = min(TOKEN_TILE, l)
    cos, sin = _mla_rope_tables(l, use_rope)
    tok = lambda w: pl.BlockSpec((1, tm, w), lambda i, j: (i, j, 0))
    return pl.pallas_call(
        _even_proj_kernel,
        grid=(b, l // tm),
        in_specs=[tok(d),
                  pl.BlockSpec((1, 2, d), lambda i, j: (i, 0, 0)),
                  _resident(vec.shape), _resident(win.shape), _resident(wuk.shape), _resident(wuv.shape),
                  _resident(wuq.shape), _resident(gmat.shape),
                  pl.BlockSpec((tm, LANES), lambda i, j: (j, 0)),
                  pl.BlockSpec((tm, LANES), lambda i, j: (j, 0))],
        out_specs=[tok(HEADS_W), tok(HEADS_W), tok(HEADS_W), tok(CONV_CH)],
        out_shape=[jax.ShapeDtypeStruct((b, l, HEADS_W), BF16)] * 3 + [jax.ShapeDtypeStruct((b, l, CONV_CH), F32)],
        compiler_params=_params("parallel", "parallel"),
    )(x, mod2, vec, win, wuk, wuv, wuq, gmat, cos, sin)


O_CK = 0
O_CV = O_CK + GQA_KV_HEADS * HEAD_DIM
O_NK = O_CV + GQA_KV_HEADS * LANES
O_NV = O_NK + NA_HEADS * HEAD_DIM
O_CQ = O_NV + NA_HEADS * LANES
O_NQ = O_CQ + GQA_HEADS * HEAD_DIM
O_COLS = O_NQ + NA_HEADS * HEAD_DIM
GQA_ORDER = (0, 4, 1, 5, 2, 6, 3, 7)


def _rope64(x, cos, sin):
    outs = []
    for p in range(x.shape[-1] // LANES):
        xs = x[:, p * LANES:(p + 1) * LANES]
        outs.append(xs * cos + _swap_halves(xs, HEAD_DIM // 2, 0) * sin)
    return outs[0] if len(outs) == 1 else jnp.concatenate(outs, axis=-1)


def _odd_proj_kernel(x_ref, mod_ref, vec_ref, win_ref, g_ref, c_ref, s_ref,
                     cq_ref, ck_ref, cv_ref, nq_ref, nk_ref, nv_ref):
    x = x_ref[0]
    mod = mod_ref[0]
    h = _rms_rows(x) * vec_ref[0:1, :]
    h = (h * (1.0 + mod[1:2]) + mod[0:1]).astype(BF16)
    pp = jnp.dot(h, win_ref[...], preferred_element_type=F32)
    cos = c_ref[...]
    sin = s_ref[...]
    g = g_ref[...]
    inv = 1.0 / HEAD_DIM

    def head_norm(v, gain):
        if v.shape[-1] == LANES:
            ss = jnp.dot((v * v).astype(BF16), g[0:LANES, 0:LANES], preferred_element_type=F32)
        else:
            ss = _seg_sumsq(v, g)
        return v * lax.rsqrt(ss * inv + EPS) * gain

    wq = GQA_HEADS * HEAD_DIM
    ck = head_norm(pp[:, O_CK:O_CV], vec_ref[1:2, 0:LANES])
    ck_ref[0] = _rope64(ck, cos, sin).astype(BF16)
    cv_ref[0] = (pp[:, O_CV:O_NK] + vec_ref[5:6, 0:GQA_KV_HEADS * LANES]).astype(BF16)
    nk_ref[0] = head_norm(pp[:, O_NK:O_NV], vec_ref[2:3, 0:wq]).astype(BF16)
    nv_ref[0] = (pp[:, O_NV:O_CQ] + vec_ref[5:6, :]).astype(BF16)
    cq = head_norm(pp[:, O_CQ:O_NQ], vec_ref[3:4, 0:wq])
    cq_ref[0] = _rope64(cq, cos, sin).astype(BF16)
    nq_ref[0] = head_norm(pp[:, O_NQ:O_COLS], vec_ref[4:5, 0:wq]).astype(BF16)


def _odd_weights(prm):
    d = D_MODEL
    ev = np.arange(0, HEAD_DIM, 2)
    od = np.arange(1, HEAD_DIM, 2)
    kvw = GQA_KV_HEADS * HEAD_DIM
    naw = NA_HEADS * HEAD_DIM
    zero_col = prm["w_in"].shape[1]
    pad64 = np.full(LANES - HEAD_DIM, zero_col)
    q0 = 2 * kvw + 2 * naw
    cols = np.concatenate(
        [np.concatenate([g * HEAD_DIM + ev, g * HEAD_DIM + od]) for g in range(GQA_KV_HEADS)]
        + [np.concatenate([kvw + g * HEAD_DIM + np.arange(HEAD_DIM), pad64]) for g in range(GQA_KV_HEADS)]
        + [2 * kvw + np.arange(naw)]
        + [np.concatenate([2 * kvw + naw + h * HEAD_DIM + np.arange(HEAD_DIM), pad64]) for h in range(NA_HEADS)]
        + [np.concatenate([q0 + h * HEAD_DIM + ev, q0 + h * HEAD_DIM + od]) for h in GQA_ORDER]
        + [q0 + GQA_HEADS * HEAD_DIM + np.arange(naw)])
    assert cols.shape[0] == O_COLS
    w_ext = jnp.concatenate([prm["w_in"], jnp.zeros((d, 1), F32)], axis=1)
    win = jnp.take(w_ext, cols, axis=1).astype(BF16)

    def row(v):
        return jnp.pad(v.astype(F32), (0, HEADS_W - v.shape[0]))

    perm = np.concatenate([ev, od])
    q_scale = (HEAD_DIM ** -0.5) * LOG2E
    ones_lane = np.tile((np.arange(LANES) == HEAD_DIM).astype(np.float32), NA_HEADS)
    vec = jnp.stack([
        row(prm["mix_norm"]),
        row(jnp.tile(prm["gqa_k_gain"][perm], GQA_KV_HEADS)),
        row(jnp.tile(prm["na_k_gain"], NA_HEADS)),
        row(jnp.tile(prm["gqa_q_gain"][perm], GQA_HEADS) * q_scale),
        row(jnp.tile(prm["na_q_gain"], NA_HEADS) * q_scale),
        jnp.asarray(ones_lane),
    ] + [jnp.zeros((HEADS_W,), F32)] * 2)
    seg_id = np.arange(MXU_DIM) // HEAD_DIM
    gmat = jnp.asarray((seg_id[:, None] == seg_id[None, :]).astype(np.float32)).astype(BF16)
    return win, vec, gmat


def _gqa_rope_tables(n_tokens, use_rope):
    if not use_rope:
        c = jnp.ones((n_tokens, LANES), F32)
        return c, jnp.zeros_like(c)
    cos, sin = _axial_angles(n_tokens, HEAD_DIM)
    return jnp.concatenate([cos, cos, cos, cos], axis=-1), jnp.concatenate([-sin, sin, -sin, sin], axis=-1)


def _odd_proj(x, mod2, weights, use_rope):
    win, vec, gmat = weights
    b, l, d = x.shape
    tm = min(TOKEN_TILE, l)
    cos, sin = _gqa_rope_tables(l, use_rope)
    tok = lambda w: pl.BlockSpec((1, tm, w), lambda i, j: (i, j, 0))
    widths = (GQA_HEADS * HEAD_DIM, GQA_KV_HEADS * HEAD_DIM, GQA_KV_HEADS * LANES,
              NA_HEADS * HEAD_DIM, NA_HEADS * HEAD_DIM, NA_HEADS * LANES)
    return pl.pallas_call(
        _odd_proj_kernel,
        grid=(b, l // tm),
        in_specs=[tok(d),
                  pl.BlockSpec((1, 2, d), lambda i, j: (i, 0, 0)),
                  _resident(vec.shape), _resident(win.shape), _resident(gmat.shape),
                  pl.BlockSpec((tm, LANES), lambda i, j: (j, 0)),
                  pl.BlockSpec((tm, LANES), lambda i, j: (j, 0))],
        out_specs=[tok(w) for w in widths],
        out_shape=[jax.ShapeDtypeStruct((b, l, w), BF16) for w in widths],
        compiler_params=_params("parallel", "parallel"),
    )(x, mod2, vec, win, gmat, cos, sin)


def _pair_queries(q, packed):
    if not packed:
        return q[:, 0:LANES], q[:, LANES:2 * LANES]
    lane = lax.broadcasted_iota(jnp.int32, q.shape, 1)
    zero = jnp.zeros_like(q)
    return jnp.where(lane < HEAD_DIM, q, zero), jnp.where(lane >= HEAD_DIM, q, zero)


def _pair_output(acc_a, acc_b):
    oa = acc_a / acc_a[:, HEAD_DIM:HEAD_DIM + 1]
    ob = acc_b / acc_b[:, HEAD_DIM:HEAD_DIM + 1]
    lane = lax.broadcasted_iota(jnp.int32, oa.shape, 1)
    return jnp.where(lane < HEAD_DIM, oa, pltpu.roll(ob, HEAD_DIM, 1))


def _score(q, k):
    return lax.dot_general(q, k, (((1,), (1,)), ((), ())), preferred_element_type=F32)


def _attn_kernel(*refs, n_src, packed, src_len):
    q_ref = refs[0]
    kv_refs = refs[1:1 + 2 * n_src]
    o_ref = refs[1 + 2 * n_src]
    qs = _pair_queries(q_ref[0], packed)
    rows = qs[0].shape[0]
    m = [jnp.full((rows, 1), NEG_BIG, F32) for _ in range(2)]
    acc = [jnp.zeros((rows, LANES), F32) for _ in range(2)]
    for s in range(n_src):
        k_ref, v_ref = kv_refs[2 * s], kv_refs[2 * s + 1]
        tk = min(ATT_TK, src_len[s])
        for blk in range(src_len[s] // tk):
            rs = slice(blk * tk, (blk + 1) * tk)
            for hd in range(2):
                ks = slice(0, LANES) if packed else slice(hd * LANES, (hd + 1) * LANES)
                sc = _score(qs[hd], k_ref[0, rs, ks])
                m_new = jnp.maximum(m[hd], jnp.max(sc, axis=-1, keepdims=True))
                p = jnp.exp2(sc - m_new).astype(BF16)
                pv = jnp.dot(p, v_ref[0, rs, hd * LANES:(hd + 1) * LANES], preferred_element_type=F32)
                acc[hd] = acc[hd] * jnp.exp2(m[hd] - m_new) + pv
                m[hd] = m_new
    o_ref[0] = _pair_output(acc[0], acc[1]).astype(o_ref.dtype)


def _pair_attention(q, sources, packed):
    b, lq, qw = q.shape
    wq = LANES if packed else 2 * LANES
    n_pairs = qw // wq
    tq = min(ATT_TQ, lq)
    in_specs = [pl.BlockSpec((1, tq, wq), lambda i, p, j: (i, j, p))]
    args = [q]
    for k, v in sources:
        lk = k.shape[1]
        if packed:
            in_specs.append(pl.BlockSpec((1, lk, LANES), lambda i, p, j: (i, 0, 0)))
            in_specs.append(pl.BlockSpec((1, lk, 2 * LANES), lambda i, p, j: (i, 0, 0)))
        else:
            in_specs.append(pl.BlockSpec((1, lk, 2 * LANES), lambda i, p, j: (i, 0, p)))
            in_specs.append(pl.BlockSpec((1, lk, 2 * LANES), lambda i, p, j: (i, 0, p)))
        args += [k, v]
    kern = functools.partial(_attn_kernel, n_src=len(sources), packed=packed,
                             src_len=tuple(k.shape[1] for k, _ in sources))
    return pl.pallas_call(
        kern,
        grid=(b, n_pairs, lq // tq),
        in_specs=in_specs,
        out_specs=pl.BlockSpec((1, tq, LANES), lambda i, p, j: (i, j, p)),
        out_shape=jax.ShapeDtypeStruct((b, lq, n_pairs * LANES), BF16),
        compiler_params=_params("parallel", "parallel", "arbitrary"),
    )(*args)


def _na_ctx_kernel(q_ref, k_ref, v_ref, o_ref):
    qs = _pair_queries(q_ref[0], True)
    accs = []
    for hd in range(2):
        sc = _score(qs[hd], k_ref[0])
        p = jnp.exp2(sc - jnp.max(sc, axis=-1, keepdims=True)).astype(BF16)
        accs.append(jnp.dot(p, v_ref[0, :, hd * LANES:(hd + 1) * LANES], preferred_element_type=F32))
    o_ref[0] = _pair_output(accs[0], accs[1]).astype(o_ref.dtype)


def _na_ctx_attention(q, k, v):
    b, l, w = q.shape
    n_pairs = w // LANES
    return pl.pallas_call(
        _na_ctx_kernel,
        grid=(b, n_pairs),
        in_specs=[pl.BlockSpec((1, l, LANES), lambda i, p: (i, 0, p)),
                  pl.BlockSpec((1, l, LANES), lambda i, p: (i, 0, p)),
                  pl.BlockSpec((1, l, 2 * LANES), lambda i, p: (i, 0, p))],
        out_specs=pl.BlockSpec((1, l, LANES), lambda i, p: (i, 0, p)),
        out_shape=jax.ShapeDtypeStruct((b, l, w), BF16),
        compiler_params=_params("parallel", "parallel"),
    )(q, k, v)


def _na_kernel(q_ref, k_ref, v_ref, kc_ref, vc_ref, bias_ref, o_ref, *, n_blocks, grid_rows):
    blk_q = NA_ROWS * GRID_W
    win = NA_WIN_ROWS * GRID_W
    kc = kc_ref[0]

    def body(i, carry):
        q0 = pl.multiple_of(i * blk_q, blk_q)
        ws = jnp.clip(i * NA_ROWS - WIN_H // 2, 0, grid_rows - NA_WIN_ROWS)
        k0 = pl.multiple_of(ws * GRID_W, GRID_W)
        cfg = jnp.where(i == 0, 0, jnp.where(i == n_blocks - 1, 2, 1))
        qs = _pair_queries(q_ref[0, pl.ds(q0, blk_q), :], True)
        kw = k_ref[0, pl.ds(k0, win), :]
        accs = []
        for hd in range(2):
            s_nb = _score(qs[hd], kw) + bias_ref[0, cfg, hd]
            s_c = _score(qs[hd], kc)
            m = jnp.maximum(jnp.max(s_nb, axis=-1, keepdims=True), jnp.max(s_c, axis=-1, keepdims=True))
            p_nb = jnp.exp2(s_nb - m).astype(BF16)
            p_c = jnp.exp2(s_c - m).astype(BF16)
            vs = slice(hd * LANES, (hd + 1) * LANES)
            accs.append(jnp.dot(p_nb, v_ref[0, pl.ds(k0, win), vs], preferred_element_type=F32)
                        + jnp.dot(p_c, vc_ref[0, :, vs], preferred_element_type=F32))
        o_ref[0, pl.ds(q0, blk_q), :] = _pair_output(accs[0], accs[1]).astype(o_ref.dtype)
        return carry

    lax.fori_loop(0, n_blocks, body, 0, unroll=NA_UNROLL)


def _na_bias_table(rpb, grid_rows):
    n_blocks = grid_rows // NA_ROWS
    assert grid_rows % NA_ROWS == 0 and grid_rows >= NA_WIN_ROWS + NA_ROWS
    n_dr, n_dc = 2 * WIN_H - 1, 2 * WIN_W - 1
    ri = np.arange(NA_ROWS)[:, None]
    kj = np.arange(NA_WIN_ROWS)[None, :]
    row_sel, row_valid = [], []
    for blk in (0, 1, n_blocks - 1):
        ws = int(np.clip(blk * NA_ROWS - WIN_H // 2, 0, grid_rows - NA_WIN_ROWS))
        r = blk * NA_ROWS + ri
        krow = ws + kj
        rs = np.clip(r - WIN_H // 2, 0, grid_rows - WIN_H)
        row_valid.append((krow >= rs) & (krow < rs + WIN_H))
        row_sel.append(np.eye(n_dr, dtype=np.float32)[np.clip(krow - r + (WIN_H - 1), 0, n_dr - 1)])
    row_sel = np.stack(row_sel)
    row_valid = np.stack(row_valid)
    cq = np.arange(GRID_W)[:, None]
    ck = np.arange(GRID_W)[None, :]
    cs = np.clip(cq - WIN_W // 2, 0, GRID_W - WIN_W)
    col_valid = (ck >= cs) & (ck < cs + WIN_W)
    col_sel = np.eye(n_dc, dtype=np.float32)[np.clip(ck - cq + (WIN_W - 1), 0, n_dc - 1)]
    rpb2 = rpb.reshape(rpb.shape[0] // 2, 2, n_dr, n_dc) * LOG2E
    t = jnp.einsum("cijd,pade->pcaije", row_sel, rpb2, precision=lax.Precision.HIGHEST)
    t = jnp.einsum("pcaije,qke->pcaiqjk", t, col_sel, precision=lax.Precision.HIGHEST)
    valid = row_valid[:, None, :, None, :, None] & col_valid[None, None, None, :, None, :]
    tab = jnp.where(jnp.asarray(valid)[None], t, NEG_BIG)
    return tab.reshape(tab.shape[:3] + (NA_ROWS * GRID_W, NA_WIN_ROWS * GRID_W))


def _neighbourhood_attention(q, k, v, kc, vc, bias):
    b, s, w = q.shape
    n_pairs = w // LANES
    grid_rows = s // GRID_W
    n_blocks = grid_rows // NA_ROWS
    lc = kc.shape[1]
    return pl.pallas_call(
        functools.partial(_na_kernel, n_blocks=n_blocks, grid_rows=grid_rows),
        grid=(b, n_pairs),
        in_specs=[pl.BlockSpec((1, s, LANES), lambda i, p: (i, 0, p)),
                  pl.BlockSpec((1, s, LANES), lambda i, p: (i, 0, p)),
                  pl.BlockSpec((1, s, 2 * LANES), lambda i, p: (i, 0, p)),
                  pl.BlockSpec((1, lc, LANES), lambda i, p: (i, 0, p)),
                  pl.BlockSpec((1, lc, 2 * LANES), lambda i, p: (i, 0, p)),
                  pl.BlockSpec((1,) + bias.shape[1:], lambda i, p: (p, 0, 0, 0, 0))],
        out_specs=pl.BlockSpec((1, s, LANES), lambda i, p: (i, 0, p)),
        out_shape=jax.ShapeDtypeStruct((b, s, w), BF16),
        compiler_params=_params("parallel", "arbitrary"),
    )(q, k, v, kc, vc, bias)


def _conv_kernel(y_ref, w_ref, vec_ref, o_ref, pad_ref, shift_ref, *, seq):
    ch = y_ref.shape[-1]
    zeros = jnp.zeros((CONV_HALO, ch), F32)
    pad_ref[0:CONV_HALO, :] = zeros
    pad_ref[CONV_HALO + seq:2 * CONV_HALO + seq, :] = zeros
    pad_ref[CONV_HALO:CONV_HALO + seq, :] = y_ref[0]
    first = CONV_HALO - CONV_WIDTH // 2

    span = CONV_TILE + 2 * CONV_HALO - SUBLANES

    def body(i, carry):
        base = pl.multiple_of(i * CONV_TILE, CONV_TILE)
        window = pad_ref[pl.ds(base, CONV_TILE + 2 * CONV_HALO), :]
        for r in range(1, SUBLANES):
            shift_ref[r - 1] = window[r:r + span, :]
        acc = jnp.zeros((CONV_TILE, ch), F32)
        for r in range(SUBLANES):
            for a in range((2 * CONV_HALO) // SUBLANES):
                t = SUBLANES * a + r - first
                if 0 <= t < CONV_WIDTH:
                    rows = slice(SUBLANES * a, SUBLANES * a + CONV_TILE)
                    tap = window[rows, :] if r == 0 else shift_ref[r - 1, rows, :]
                    acc = acc + tap * w_ref[t:t + 1, :]
        acc = acc + vec_ref[0:1, :]
        mu = jnp.mean(acc, axis=-1, keepdims=True)
        cen = acc - mu
        var = jnp.mean(cen * cen, axis=-1, keepdims=True)
        z = cen * lax.rsqrt(var + EPS) * vec_ref[1:2, :] + vec_ref[2:3, :]
        o_ref[0, pl.ds(base, CONV_TILE), :] = _silu(z).astype(o_ref.dtype)
        return carry

    lax.fori_loop(0, seq // CONV_TILE, body, 0)


def _conv_module(y, prm):
    b, l, ch = y.shape
    vec = jnp.stack([prm["conv_dw_b"], prm["conv_ln_g"], prm["conv_ln_b"]] + [jnp.zeros((ch,), F32)] * 5)
    w = jnp.pad(prm["conv_dw_w"], ((0, 1), (0, 0)))
    return pl.pallas_call(
        functools.partial(_conv_kernel, seq=l),
        grid=(b,),
        in_specs=[pl.BlockSpec((1, l, ch), lambda i: (i, 0, 0)), _resident(w.shape), _resident(vec.shape)],
        out_specs=pl.BlockSpec((1, l, ch), lambda i: (i, 0, 0)),
        out_shape=jax.ShapeDtypeStruct((b, l, ch), BF16),
        scratch_shapes=[pltpu.VMEM((l + 2 * CONV_HALO, ch), F32),
                        pltpu.VMEM((SUBLANES - 1, CONV_TILE + 2 * CONV_HALO - SUBLANES, ch), F32)],
        compiler_params=_params("parallel"),
    )(y, w, vec)


def _flat_ctx(a, batch):
    if a.shape[0] == batch:
        return a.reshape(1, batch * a.shape[1], a.shape[2])
    return a.reshape(batch, a.shape[1] // batch, a.shape[2])


def _trunk_layer(xl, xc, mods, prm, even, ctx_out):
    batch = xl.shape[0]
    mod_l, mod_c = mods
    xl = _half_ffn(xl, mod_l[:, 0:3], prm["ffn1_norm"], prm["ffn1_w_in"], prm["ffn1_w_out"])
    xc = _half_ffn(xc, mod_c[:, 0:3], prm["ffn1_norm"], prm["ffn1_w_in"], prm["ffn1_w_out"])
    w_out = prm["w_out"]
    half = w_out.shape[0] // 2
    parts_c = None
    if even:
        weights = _even_weights(prm)
        ql, kl, vl, yl = _even_proj(xl, mod_l[:, 3:5], weights, True)
        qc, kc, vc, yc = _even_proj(xc, mod_c[:, 3:5], weights, False)
        qc, kc, vc, yc = (_flat_ctx(a, batch) for a in (qc, kc, vc, yc))
        parts_l = (_pair_attention(ql, [(kc, vc), (kl, vl)], packed=False), _conv_module(yl, prm))
        w_a, w_b = w_out[:half], w_out[half:]
        if ctx_out:
            parts_c = (_pair_attention(qc, [(kc, vc)], packed=False), _conv_module(yc, prm))
    else:
        weights = _odd_weights(prm)
        cql, ckl, cvl, nql, nkl, nvl = _odd_proj(xl, mod_l[:, 3:5], weights, True)
        ctx_parts = [_flat_ctx(a, batch) for a in _odd_proj(xc, mod_c[:, 3:5], weights, False)]
        cqc, ckc, cvc, nqc, nkc, nvc = ctx_parts
        bias = _na_bias_table(prm["na_rpb"], xl.shape[1] // GRID_W)
        parts_l = (_pair_attention(cql, [(ckc, cvc), (ckl, cvl)], packed=True),
                   _neighbourhood_attention(nql, nkl, nvl, nkc, nvc, bias))
        order = np.concatenate([h * HEAD_DIM + np.arange(HEAD_DIM) for h in GQA_ORDER])
        w_a, w_b = w_out[:half][order], w_out[half:]
        if ctx_out:
            parts_c = (_pair_attention(cqc, [(ckc, cvc)], packed=True), _na_ctx_attention(nqc, nkc, nvc))

    def mixer_and_ffn2(x, mod, parts):
        mod4 = jnp.concatenate([mod[:, 6:9], mod[:, 5:6]], axis=1)
        return _half_ffn(x, mod4, prm["ffn2_norm"], prm["ffn2_w_in"], prm["ffn2_w_out"], mixer=parts + (w_a, w_b))

    xl = mixer_and_ffn2(xl, mod_l, parts_l)
    if ctx_out:
        xc = mixer_and_ffn2(xc, mod_c, tuple(_flat_ctx(p, batch) for p in parts_c))
    return xl, xc


def _layer_mods(c, c_ctx, prm):
    batch, d = c.shape
    rows = -(-(batch + 1) // 8) * 8
    cond = jnp.zeros((rows, d), F32).at[:batch].set(c).at[batch].set(c_ctx)
    mod = _modulation(cond, prm["mod_w"], prm["mod_b"]).reshape(rows, N_MOD, d)
    return mod[:batch], mod[batch:batch + 1]


def kernel(x, c, ctx, c_ctx,
           l0_mod_w, l0_mod_b, l0_ffn1_norm, l0_ffn1_w_in, l0_ffn1_w_out, l0_mix_norm, l0_w_in,
           l0_mla_q_norm, l0_mla_w_uq, l0_mla_kv_norm, l0_mla_w_ukv, l0_mla_q_gain, l0_mla_k_gain,
           l0_conv_glu_b, l0_conv_dw_w, l0_conv_dw_b, l0_conv_ln_g, l0_conv_ln_b,
           l0_w_out, l0_ffn2_norm, l0_ffn2_w_in, l0_ffn2_w_out,
           l1_mod_w, l1_mod_b, l1_ffn1_norm, l1_ffn1_w_in, l1_ffn1_w_out, l1_mix_norm, l1_w_in,
           l1_gqa_q_gain, l1_gqa_k_gain, l1_na_q_gain, l1_na_k_gain, l1_na_rpb,
           l1_w_out, l1_ffn2_norm, l1_ffn2_w_in, l1_ffn2_w_out):
    layers = (
        dict(mod_w=l0_mod_w, mod_b=l0_mod_b, ffn1_norm=l0_ffn1_norm, ffn1_w_in=l0_ffn1_w_in,
             ffn1_w_out=l0_ffn1_w_out, mix_norm=l0_mix_norm, w_in=l0_w_in,
             mla_q_norm=l0_mla_q_norm, mla_w_uq=l0_mla_w_uq, mla_kv_norm=l0_mla_kv_norm,
             mla_w_ukv=l0_mla_w_ukv, mla_q_gain=l0_mla_q_gain, mla_k_gain=l0_mla_k_gain,
             conv_glu_b=l0_conv_glu_b, conv_dw_w=l0_conv_dw_w, conv_dw_b=l0_conv_dw_b,
             conv_ln_g=l0_conv_ln_g, conv_ln_b=l0_conv_ln_b, w_out=l0_w_out,
             ffn2_norm=l0_ffn2_norm, ffn2_w_in=l0_ffn2_w_in, ffn2_w_out=l0_ffn2_w_out),
        dict(mod_w=l1_mod_w, mod_b=l1_mod_b, ffn1_norm=l1_ffn1_norm, ffn1_w_in=l1_ffn1_w_in,
             ffn1_w_out=l1_ffn1_w_out, mix_norm=l1_mix_norm, w_in=l1_w_in,
             gqa_q_gain=l1_gqa_q_gain, gqa_k_gain=l1_gqa_k_gain, na_q_gain=l1_na_q_gain,
             na_k_gain=l1_na_k_gain, na_rpb=l1_na_rpb, w_out=l1_w_out,
             ffn2_norm=l1_ffn2_norm, ffn2_w_in=l1_ffn2_w_in, ffn2_w_out=l1_ffn2_w_out),
    )
    batch = x.shape[0]
    xl, xc = x, _flat_ctx(ctx, batch)
    for i, prm in enumerate(layers):
        mods = _layer_mods(c, c_ctx, prm)
        xl, xc = _trunk_layer(xl, xc, mods, prm, even=(i % 2 == 0), ctx_out=(i < len(layers) - 1))
    return xl
```

```python
import functools
import math

import numpy as np
import jax
import jax.numpy as jnp
from jax import lax
from jax.experimental import pallas as pl
from jax.experimental.pallas import tpu as pltpu

F32 = jnp.float32
BF16 = jnp.bfloat16

D_MODEL = 1024
CTX_LEN = 256
GRID_W = 64
D_FF = 2816
N_MOD = 9
EPS = 1e-6
ROPE_BASE = 10000.0
MLA_HEADS = 8
MLA_NOPE = 64
MLA_ROPE = 32
MLA_V = 64
MLA_Q_LORA = 384
MLA_KV_LORA = 256
CONV_CH = 512
CONV_WIDTH = 31
HEAD_DIM = 64
GQA_HEADS = 8
GQA_KV_HEADS = 2
NA_HEADS = 8
WIN_H = 8
WIN_W = 16

LANES = 128
SUBLANES = 8
MXU_DIM = 256
VMEM_LIMIT = 56 * 1024 * 1024
LOG2E = math.log2(math.e)
NEG_BIG = -1e30

FFN_CHUNK = 256
TOKEN_TILE = 512
FFN_TILE = 1024
ATT_TQ = 512
ATT_TK = 256
NA_ROWS = 4
NA_WIN_ROWS = 12
NA_GROUP = 4
NA_UNROLL = 2
CONV_TILE = 128
CONV_HALO = 16


def _params(*sem):
    return pltpu.CompilerParams(dimension_semantics=sem, vmem_limit_bytes=VMEM_LIMIT)


def _resident(shape):
    return pl.BlockSpec(shape, lambda *_: (0,) * len(shape), pipeline_mode=pl.Buffered(1))


def _silu(x):
    return x * jax.nn.sigmoid(x)


def _rms_rows(x):
    return x * lax.rsqrt(jnp.mean(x * x, axis=-1, keepdims=True) + EPS)


def _seg_sumsq(x, g):
    outs = []
    for p in range(x.shape[-1] // MXU_DIM):
        sq = x[:, p * MXU_DIM:(p + 1) * MXU_DIM]
        outs.append(jnp.dot((sq * sq).astype(BF16), g, preferred_element_type=F32))
    return outs[0] if len(outs) == 1 else jnp.concatenate(outs, axis=-1)


def _swap_halves(x, half, lo):
    lane = lax.broadcasted_iota(jnp.int32, x.shape, 1)
    first = ((lane - lo) & (2 * half - 1)) < half
    return jnp.where(first, pltpu.roll(x, LANES - half, 1), pltpu.roll(x, half, 1))


def _mod_kernel(c_ref, w_ref, b_ref, o_ref):
    a = _silu(c_ref[...]).astype(BF16)
    o_ref[...] = jnp.dot(a, w_ref[...].astype(BF16), preferred_element_type=F32) + b_ref[...]


def _modulation(cond, mod_w, mod_b):
    rows, d = cond.shape
    n = mod_w.shape[1]
    return pl.pallas_call(
        _mod_kernel,
        grid=(n // d,),
        in_specs=[pl.BlockSpec((rows, d), lambda j: (0, 0)),
                  pl.BlockSpec((d, d), lambda j: (0, j)),
                  pl.BlockSpec((1, d), lambda j: (0, j))],
        out_specs=pl.BlockSpec((rows, d), lambda j: (0, j)),
        out_shape=jax.ShapeDtypeStruct((rows, n), F32),
        compiler_params=_params("arbitrary"),
    )(cond, mod_w, mod_b.reshape(1, n))


def _ffn_kernel(*refs, mixer):
    if mixer:
        x_ref, mod_ref, g_ref, win_ref, wout_ref, pa_ref, pb_ref, wa_ref, wb_ref, o_ref, h_ref, a_ref = refs
    else:
        x_ref, mod_ref, g_ref, win_ref, wout_ref, o_ref, h_ref, a_ref = refs
    x = x_ref[0]
    mod = mod_ref[0]
    if mixer:
        mix = (jnp.dot(pa_ref[0], wa_ref[...], preferred_element_type=F32)
               + jnp.dot(pb_ref[0], wb_ref[...], preferred_element_type=F32))
        x = x + mod[3:4] * mix
    h = _rms_rows(x) * g_ref[...]
    h_ref[...] = (h * (1.0 + mod[1:2]) + mod[0:1]).astype(BF16)
    for c in range(D_FF // FFN_CHUNK):
        gate = jnp.dot(h_ref[...], win_ref[:, c * FFN_CHUNK:(c + 1) * FFN_CHUNK], preferred_element_type=F32)
        up = jnp.dot(h_ref[...], win_ref[:, D_FF + c * FFN_CHUNK:D_FF + (c + 1) * FFN_CHUNK], preferred_element_type=F32)
        a_ref[:, c * FFN_CHUNK:(c + 1) * FFN_CHUNK] = (_silu(gate) * up).astype(BF16)
    y = jnp.dot(a_ref[...], wout_ref[...], preferred_element_type=F32)
    o_ref[0] = x + (0.5 * mod[2:3]) * y


def _half_ffn(x, mod, norm_g, w_in, w_out, mixer=None):
    b, l, d = x.shape
    tm = min(FFN_TILE, l)
    tok = lambda w: pl.BlockSpec((1, tm, w), lambda i, j: (i, j, 0))
    in_specs = [tok(d), pl.BlockSpec((1,) + mod.shape[1:], lambda i, j: (i, 0, 0)),
                _resident((1, d)), _resident(w_in.shape), _resident(w_out.shape)]
    args = [x, mod, norm_g.reshape(1, d), w_in.astype(BF16), w_out.astype(BF16)]
    if mixer is not None:
        part_a, part_b, w_a, w_b = mixer
        in_specs += [tok(part_a.shape[-1]), tok(part_b.shape[-1]), _resident(w_a.shape), _resident(w_b.shape)]
        args += [part_a, part_b, w_a.astype(BF16), w_b.astype(BF16)]
    return pl.pallas_call(
        functools.partial(_ffn_kernel, mixer=mixer is not None),
        grid=(b, l // tm),
        in_specs=in_specs,
        out_specs=tok(d),
        out_shape=jax.ShapeDtypeStruct((b, l, d), F32),
        scratch_shapes=[pltpu.VMEM((tm, d), BF16), pltpu.VMEM((tm, D_FF), BF16)],
        compiler_params=_params("parallel", "parallel"),
    )(*args)


E_KR = MLA_KV_LORA
E_CQ = E_KR + LANES
E_CA = E_CQ + MLA_Q_LORA
E_CG = E_CA + CONV_CH
E_COLS = E_CG + CONV_CH
HEADS_W = MLA_HEADS * LANES


def _even_proj_kernel(x_ref, mod_ref, vec_ref, win_ref, wuk_ref, wuv_ref, wuq_ref, g_ref, c_ref, s_ref,
                      q_ref, k_ref, v_ref, y_ref):
    x = x_ref[0]
    mod = mod_ref[0]
    h = _rms_rows(x) * vec_ref[0:1, :]
    h = (h * (1.0 + mod[1:2]) + mod[0:1]).astype(BF16)
    pp = jnp.dot(h, win_ref[...], preferred_element_type=F32)
    cos = c_ref[...]
    sin = s_ref[...]
    g = g_ref[...]

    cq = (_rms_rows(pp[:, E_CQ:E_CA]) * vec_ref[2:3, 0:MLA_Q_LORA]).astype(BF16)
    q = jnp.dot(cq, wuq_ref[...], preferred_element_type=F32)
    q = q * lax.rsqrt(_seg_sumsq(q, g) * vec_ref[6:7, :] + EPS) * vec_ref[5:6, :]
    for hd in range(MLA_HEADS):
        sl = slice(hd * LANES, (hd + 1) * LANES)
        qh = q[:, sl]
        qh = qh * cos + _swap_halves(qh, MLA_ROPE // 2, MLA_NOPE) * sin
        q_ref[0, :, sl] = qh.astype(BF16)

    ckv = (_rms_rows(pp[:, 0:E_KR]) * vec_ref[1:2, 0:MLA_KV_LORA]).astype(BF16)
    kn = jnp.dot(ckv, wuk_ref[...], preferred_element_type=F32)
    vv = jnp.dot(ckv, wuv_ref[...], preferred_element_type=F32)
    kn = kn * lax.rsqrt(_seg_sumsq(kn, g) * vec_ref[6:7, :] + EPS) * vec_ref[3:4, :]
    kr = pp[:, E_KR:E_CQ]
    kr = kr * lax.rsqrt(jnp.sum(kr * kr, axis=-1, keepdims=True) * (1.0 / MLA_ROPE) + EPS) * vec_ref[4:5, 0:LANES]
    kr = kr * cos + _swap_halves(kr, MLA_ROPE // 2, MLA_NOPE) * sin
    v_ref[0] = (vv + vec_ref[8:9, :]).astype(BF16)
    for hd in range(MLA_HEADS):
        sl = slice(hd * LANES, (hd + 1) * LANES)
        k_ref[0, :, sl] = (kn[:, sl] + kr).astype(BF16)

    glu_b = vec_ref[7:8, :]
    y_ref[0] = (pp[:, E_CA:E_CG] + glu_b[:, 0:CONV_CH]) * jax.nn.sigmoid(pp[:, E_CG:E_COLS] + glu_b[:, CONV_CH:])


def _even_weights(prm):
    d = D_MODEL
    ev = np.arange(0, MLA_ROPE, 2)
    od = np.arange(1, MLA_ROPE, 2)
    zero_col = prm["w_in"].shape[1]
    cols = np.concatenate([
        np.arange(MLA_KV_LORA),
        np.full(MLA_NOPE, zero_col), MLA_KV_LORA + ev, MLA_KV_LORA + od, np.full(LANES - MLA_NOPE - MLA_ROPE, zero_col),
        np.arange(MLA_KV_LORA + MLA_ROPE, zero_col)])
    w_ext = jnp.concatenate([prm["w_in"], jnp.zeros((d, 1), F32)], axis=1)
    win = jnp.take(w_ext, cols, axis=1).astype(BF16)

    qd = MLA_NOPE + MLA_ROPE
    zq = MLA_HEADS * qd
    qcols = np.concatenate([np.concatenate([h * qd + np.arange(MLA_NOPE), h * qd + MLA_NOPE + ev, h * qd + MLA_NOPE + od,
                                            np.full(LANES - qd, zq)]) for h in range(MLA_HEADS)])
    wuq = jnp.take(jnp.concatenate([prm["mla_w_uq"], jnp.zeros((MLA_Q_LORA, 1), F32)], axis=1), qcols, axis=1).astype(BF16)
    kvd = MLA_NOPE + MLA_V
    zk = MLA_HEADS * kvd
    kcols = np.concatenate([np.concatenate([h * kvd + np.arange(MLA_NOPE), np.full(LANES - MLA_NOPE, zk)])
                            for h in range(MLA_HEADS)])
    vcols = np.concatenate([np.concatenate([h * kvd + MLA_NOPE + np.arange(MLA_V), np.full(LANES - MLA_V, zk)])
                            for h in range(MLA_HEADS)])
    wukv = jnp.concatenate([prm["mla_w_ukv"], jnp.zeros((MLA_KV_LORA, 1), F32)], axis=1)
    wuk = jnp.take(wukv, kcols, axis=1).astype(BF16)
    wuv = jnp.take(wukv, vcols, axis=1).astype(BF16)

    def row(v):
        return jnp.pad(v.astype(F32), (0, HEADS_W - v.shape[0]))

    rope_perm = np.concatenate([MLA_NOPE + ev, MLA_NOPE + od])
    pad_q = jnp.zeros((LANES - qd,), F32)
    kg, qg = prm["mla_k_gain"], prm["mla_q_gain"]
    q_scale = (qd ** -0.5) * LOG2E
    inv_cnt = np.tile(np.concatenate([np.full(MLA_NOPE, 1.0 / MLA_NOPE), np.full(MLA_ROPE, 1.0 / MLA_ROPE),
                                      np.ones(LANES - qd)]), MLA_HEADS).astype(np.float32)
    ones_lane = np.tile((np.arange(LANES) == MLA_V).astype(np.float32), MLA_HEADS)
    vec = jnp.stack([
        row(prm["mix_norm"]),
        row(prm["mla_kv_norm"]),
        row(prm["mla_q_norm"]),
        jnp.tile(jnp.concatenate([kg[:MLA_NOPE], jnp.zeros((LANES - MLA_NOPE,), F32)]), MLA_HEADS),
        row(jnp.concatenate([jnp.zeros((MLA_NOPE,), F32), kg[rope_perm], pad_q])),
        jnp.tile(jnp.concatenate([qg[:MLA_NOPE], qg[rope_perm], pad_q]), MLA_HEADS) * q_scale,
        jnp.asarray(inv_cnt),
        row(prm["conv_glu_b"]),
        jnp.asarray(ones_lane),
    ] + [jnp.zeros((HEADS_W,), F32)] * 7)

    seg = np.arange(MXU_DIM)
    seg_id = (seg // LANES) * 4 + np.where(seg % LANES < MLA_NOPE, 0, np.where(seg % LANES < qd, 1, 2))
    gmat = jnp.asarray((seg_id[:, None] == seg_id[None, :]).astype(np.float32)).astype(BF16)
    return win, wuk, wuv, wuq, vec, gmat


def _mla_rope_tables(n_tokens, use_rope):
    ones = jnp.ones((n_tokens, MLA_NOPE), F32)
    tail = jnp.ones((n_tokens, LANES - MLA_NOPE - MLA_ROPE), F32)
    if not use_rope:
        c = jnp.ones((n_tokens, LANES), F32)
        return c, jnp.zeros_like(c)
    cos, sin = _axial_angles(n_tokens, MLA_ROPE)
    c = jnp.concatenate([ones, cos, cos, tail], axis=-1)
    s = jnp.concatenate([0 * ones, -sin, sin, 0 * tail], axis=-1)
    return c, s


def _axial_angles(n_tokens, dim):
    t = jnp.arange(n_tokens)
    row = (t // GRID_W).astype(F32)
    col = (t % GRID_W).astype(F32)
    n_pairs = dim // 4
    inv = ROPE_BASE ** (-jnp.arange(n_pairs, dtype=F32) / n_pairs)
    ang = jnp.concatenate([row[:, None] * inv, col[:, None] * inv], axis=-1)
    return jnp.cos(ang), jnp.sin(ang)


def _even_proj(x, mod2, weights, use_rope):
    win, wuk, wuv, wuq, vec, gmat = weights
    b, l, d = x.shape
    tm = min(TOKEN_TILE, l)
    cos, sin = _mla_rope_tables(l, use_rope)
    tok = lambda w: pl.BlockSpec((1, tm, w), lambda i, j: (i, j, 0))
    return pl.pallas_call(
        _even_proj_kernel,
        grid=(b, l // tm),
        in_specs=[tok(d),
                  pl.BlockSpec((1, 2, d), lambda i, j: (i, 0, 0)),
                  _resident(vec.shape), _resident(win.shape), _resident(wuk.shape), _resident(wuv.shape),
                  _resident(wuq.shape), _resident(gmat.shape),
                  pl.BlockSpec((tm, LANES), lambda i, j: (j, 0)),
                  pl.BlockSpec((tm, LANES), lambda i, j: (j, 0))],
        out_specs=[tok(HEADS_W), tok(HEADS_W), tok(HEADS_W), tok(CONV_CH)],
        out_shape=[jax.ShapeDtypeStruct((b, l, HEADS_W), BF16)] * 3 + [jax.ShapeDtypeStruct((b, l, CONV_CH), F32)],
        compiler_params=_params("parallel", "parallel"),
    )(x, mod2, vec, win, wuk, wuv, wuq, gmat, cos, sin)


O_CK = 0
O_CV = O_CK + GQA_KV_HEADS * HEAD_DIM
O_NK = O_CV + GQA_KV_HEADS * LANES
O_NV = O_NK + NA_HEADS * HEAD_DIM
O_CQ = O_NV + NA_HEADS * LANES
O_NQ = O_CQ + GQA_HEADS * HEAD_DIM
O_COLS = O_NQ + NA_HEADS * HEAD_DIM
GQA_ORDER = (0, 4, 1, 5, 2, 6, 3, 7)


def _rope64(x, cos, sin):
    outs = []
    for p in range(x.shape[-1] // LANES):
        xs = x[:, p * LANES:(p + 1) * LANES]
        outs.append(xs * cos + _swap_halves(xs, HEAD_DIM // 2, 0) * sin)
    return outs[0] if len(outs) == 1 else jnp.concatenate(outs, axis=-1)


def _odd_proj_kernel(x_ref, mod_ref, vec_ref, win_ref, g_ref, c_ref, s_ref,
                     cq_ref, ck_ref, cv_ref, nq_ref, nk_ref, nv_ref):
    x = x_ref[0]
    mod = mod_ref[0]
    h = _rms_rows(x) * vec_ref[0:1, :]
    h = (h * (1.0 + mod[1:2]) + mod[0:1]).astype(BF16)
    pp = jnp.dot(h, win_ref[...], preferred_element_type=F32)
    cos = c_ref[...]
    sin = s_ref[...]
    g = g_ref[...]
    inv = 1.0 / HEAD_DIM

    def head_norm(v, gain):
        if v.shape[-1] == LANES:
            ss = jnp.dot((v * v).astype(BF16), g[0:LANES, 0:LANES], preferred_element_type=F32)
        else:
            ss = _seg_sumsq(v, g)
        return v * lax.rsqrt(ss * inv + EPS) * gain

    wq = GQA_HEADS * HEAD_DIM
    cq = head_norm(pp[:, O_CQ:O_NQ], vec_ref[3:4, 0:wq])
    cq_ref[0] = _rope64(cq, cos, sin).astype(BF16)
    ck = head_norm(pp[:, O_CK:O_CV], vec_ref[1:2, 0:LANES])
    ck_ref[0] = _rope64(ck, cos, sin).astype(BF16)
    nq_ref[0] = head_norm(pp[:, O_NQ:O_COLS], vec_ref[4:5, 0:wq]).astype(BF16)
    nk_ref[0] = head_norm(pp[:, O_NK:O_NV], vec_ref[2:3, 0:wq]).astype(BF16)
    cv_ref[0] = (pp[:, O_CV:O_NK] + vec_ref[5:6, 0:GQA_KV_HEADS * LANES]).astype(BF16)
    nv_ref[0] = (pp[:, O_NV:O_CQ] + vec_ref[5:6, :]).astype(BF16)


def _odd_weights(prm):
    d = D_MODEL
    ev = np.arange(0, HEAD_DIM, 2)
    od = np.arange(1, HEAD_DIM, 2)
    kvw = GQA_KV_HEADS * HEAD_DIM
    naw = NA_HEADS * HEAD_DIM
    zero_col = prm["w_in"].shape[1]
    pad64 = np.full(LANES - HEAD_DIM, zero_col)
    q0 = 2 * kvw + 2 * naw
    cols = np.concatenate(
        [np.concatenate([g * HEAD_DIM + ev, g * HEAD_DIM + od]) for g in range(GQA_KV_HEADS)]
        + [np.concatenate([kvw + g * HEAD_DIM + np.arange(HEAD_DIM), pad64]) for g in range(GQA_KV_HEADS)]
        + [2 * kvw + np.arange(naw)]
        + [np.concatenate([2 * kvw + naw + h * HEAD_DIM + np.arange(HEAD_DIM), pad64]) for h in range(NA_HEADS)]
        + [np.concatenate([q0 + h * HEAD_DIM + ev, q0 + h * HEAD_DIM + od]) for h in GQA_ORDER]
        + [q0 + GQA_HEADS * HEAD_DIM + np.arange(naw)])
    assert cols.shape[0] == O_COLS
    w_ext = jnp.concatenate([prm["w_in"], jnp.zeros((d, 1), F32)], axis=1)
    win = jnp.take(w_ext, cols, axis=1).astype(BF16)

    def row(v):
        return jnp.pad(v.astype(F32), (0, HEADS_W - v.shape[0]))

    perm = np.concatenate([ev, od])
    q_scale = (HEAD_DIM ** -0.5) * LOG2E
    ones_lane = np.tile((np.arange(LANES) == HEAD_DIM).astype(np.float32), NA_HEADS)
    vec = jnp.stack([
        row(prm["mix_norm"]),
        row(jnp.tile(prm["gqa_k_gain"][perm], GQA_KV_HEADS)),
        row(jnp.tile(prm["na_k_gain"], NA_HEADS)),
        row(jnp.tile(prm["gqa_q_gain"][perm], GQA_HEADS) * q_scale),
        row(jnp.tile(prm["na_q_gain"], NA_HEADS) * q_scale),
        jnp.asarray(ones_lane),
    ] + [jnp.zeros((HEADS_W,), F32)] * 2)
    seg_id = np.arange(MXU_DIM) // HEAD_DIM
    gmat = jnp.asarray((seg_id[:, None] == seg_id[None, :]).astype(np.float32)).astype(BF16)
    return win, vec, gmat


def _gqa_rope_tables(n_tokens, use_rope):
    if not use_rope:
        c = jnp.ones((n_tokens, LANES), F32)
        return c, jnp.zeros_like(c)
    cos, sin = _axial_angles(n_tokens, HEAD_DIM)
    return jnp.concatenate([cos, cos, cos, cos], axis=-1), jnp.concatenate([-sin, sin, -sin, sin], axis=-1)


def _odd_proj(x, mod2, weights, use_rope):
    win, vec, gmat = weights
    b, l, d = x.shape
    tm = min(TOKEN_TILE, l)
    cos, sin = _gqa_rope_tables(l, use_rope)
    tok = lambda w: pl.BlockSpec((1, tm, w), lambda i, j: (i, j, 0))
    widths = (GQA_HEADS * HEAD_DIM, GQA_KV_HEADS * HEAD_DIM, GQA_KV_HEADS * LANES,
              NA_HEADS * HEAD_DIM, NA_HEADS * HEAD_DIM, NA_HEADS * LANES)
    return pl.pallas_call(
        _odd_proj_kernel,
        grid=(b, l // tm),
        in_specs=[tok(d),
                  pl.BlockSpec((1, 2, d), lambda i, j: (i, 0, 0)),
                  _resident(vec.shape), _resident(win.shape), _resident(gmat.shape),
                  pl.BlockSpec((tm, LANES), lambda i, j: (j, 0)),
                  pl.BlockSpec((tm, LANES), lambda i, j: (j, 0))],
        out_specs=[tok(w) for w in widths],
        out_shape=[jax.ShapeDtypeStruct((b, l, w), BF16) for w in widths],
        compiler_params=_params("parallel", "parallel"),
    )(x, mod2, vec, win, gmat, cos, sin)


def _pair_queries(q, packed):
    if not packed:
        return q[:, 0:LANES], q[:, LANES:2 * LANES]
    lane = lax.broadcasted_iota(jnp.int32, q.shape, 1)
    zero = jnp.zeros_like(q)
    return jnp.where(lane < HEAD_DIM, q, zero), jnp.where(lane >= HEAD_DIM, q, zero)


def _pair_output(acc_a, acc_b):
    oa = acc_a / acc_a[:, HEAD_DIM:HEAD_DIM + 1]
    ob = acc_b / acc_b[:, HEAD_DIM:HEAD_DIM + 1]
    lane = lax.broadcasted_iota(jnp.int32, oa.shape, 1)
    return jnp.where(lane < HEAD_DIM, oa, pltpu.roll(ob, HEAD_DIM, 1))


def _score(q, k):
    return lax.dot_general(q, k, (((1,), (1,)), ((), ())), preferred_element_type=F32)


def _attn_kernel(*refs, n_src, packed, src_len):
    q_ref = refs[0]
    kv_refs = refs[1:1 + 2 * n_src]
    o_ref = refs[1 + 2 * n_src]
    qs = _pair_queries(q_ref[0], packed)
    rows = qs[0].shape[0]
    m = [jnp.full((rows, 1), NEG_BIG, F32) for _ in range(2)]
    acc = [jnp.zeros((rows, LANES), F32) for _ in range(2)]
    for s in range(n_src):
        k_ref, v_ref = kv_refs[2 * s], kv_refs[2 * s + 1]
        tk = min(ATT_TK, src_len[s])
        for blk in range(src_len[s] // tk):
            rs = slice(blk * tk, (blk + 1) * tk)
            for hd in range(2):
                ks = slice(0, LANES) if packed else slice(hd * LANES, (hd + 1) * LANES)
                sc = _score(qs[hd], k_ref[0, rs, ks])
                m_new = jnp.maximum(m[hd], jnp.max(sc, axis=-1, keepdims=True))
                p = jnp.exp2(sc - m_new).astype(BF16)
                pv = jnp.dot(p, v_ref[0, rs, hd * LANES:(hd + 1) * LANES], preferred_element_type=F32)
                acc[hd] = acc[hd] * jnp.exp2(m[hd] - m_new) + pv
                m[hd] = m_new
    o_ref[0] = _pair_output(acc[0], acc[1]).astype(o_ref.dtype)


def _pair_attention(q, sources, packed):
    b, lq, qw = q.shape
    wq = LANES if packed else 2 * LANES
    n_pairs = qw // wq
    tq = min(ATT_TQ, lq)
    in_specs = [pl.BlockSpec((1, tq, wq), lambda i, p, j: (i, j, p))]
    args = [q]
    for k, v in sources:
        lk = k.shape[1]
        if packed:
            in_specs.append(pl.BlockSpec((1, lk, LANES), lambda i, p, j: (i, 0, 0)))
            in_specs.append(pl.BlockSpec((1, lk, 2 * LANES), lambda i, p, j: (i, 0, 0)))
        else:
            in_specs.append(pl.BlockSpec((1, lk, 2 * LANES), lambda i, p, j: (i, 0, p)))
            in_specs.append(pl.BlockSpec((1, lk, 2 * LANES), lambda i, p, j: (i, 0, p)))
        args += [k, v]
    kern = functools.partial(_attn_kernel, n_src=len(sources), packed=packed,
                             src_len=tuple(k.shape[1] for k, _ in sources))
    return pl.pallas_call(
        kern,
        grid=(b, n_pairs, lq // tq),
        in_specs=in_specs,
        out_specs=pl.BlockSpec((1, tq, LANES), lambda i, p, j: (i, j, p)),
        out_shape=jax.ShapeDtypeStruct((b, lq, n_pairs * LANES), BF16),
        compiler_params=_params("parallel", "parallel", "arbitrary"),
    )(*args)


def _na_ctx_kernel(q_ref, k_ref, v_ref, o_ref):
    qs = _pair_queries(q_ref[0], True)
    accs = []
    for hd in range(2):
        sc = _score(qs[hd], k_ref[0])
        p = jnp.exp2(sc - jnp.max(sc, axis=-1, keepdims=True)).astype(BF16)
        accs.append(jnp.dot(p, v_ref[0, :, hd * LANES:(hd + 1) * LANES], preferred_element_type=F32))
    o_ref[0] = _pair_output(accs[0], accs[1]).astype(o_ref.dtype)


def _na_ctx_attention(q, k, v):
    b, l, w = q.shape
    n_pairs = w // LANES
    return pl.pallas_call(
        _na_ctx_kernel,
        grid=(b, n_pairs),
        in_specs=[pl.BlockSpec((1, l, LANES), lambda i, p: (i, 0, p)),
                  pl.BlockSpec((1, l, LANES), lambda i, p: (i, 0, p)),
                  pl.BlockSpec((1, l, 2 * LANES), lambda i, p: (i, 0, p))],
        out_specs=pl.BlockSpec((1, l, LANES), lambda i, p: (i, 0, p)),
        out_shape=jax.ShapeDtypeStruct((b, l, w), BF16),
        compiler_params=_params("parallel", "parallel"),
    )(q, k, v)


def _na_kernel(q_ref, k_ref, v_ref, kc_ref, vc_ref, bias_ref, o_ref, *, n_blocks, grid_rows):
    blk_q = NA_ROWS * GRID_W
    win = NA_WIN_ROWS * GRID_W
    kc = kc_ref[0]

    def body(g, carry):
        work = []
        for u in range(NA_GROUP):
            i = g * NA_GROUP + u
            q0 = pl.multiple_of(i * blk_q, blk_q)
            ws = jnp.clip(i * NA_ROWS - WIN_H // 2, 0, grid_rows - NA_WIN_ROWS)
            k0 = pl.multiple_of(ws * GRID_W, GRID_W)
            cfg = jnp.where(i == 0, 0, jnp.where(i == n_blocks - 1, 2, 1))
            qs = _pair_queries(q_ref[0, pl.ds(q0, blk_q), :], True)
            kw = k_ref[0, pl.ds(k0, win), :]
            sc = [(_score(qs[hd], kw) + bias_ref[0, cfg, hd], _score(qs[hd], kc)) for hd in range(2)]
            work.append((q0, k0, sc))
        for q0, k0, sc in work:
            accs = []
            for hd in range(2):
                s_nb, s_c = sc[hd]
                m = jnp.maximum(jnp.max(s_nb, axis=-1, keepdims=True), jnp.max(s_c, axis=-1, keepdims=True))
                p_nb = jnp.exp2(s_nb - m).astype(BF16)
                p_c = jnp.exp2(s_c - m).astype(BF16)
                vs = slice(hd * LANES, (hd + 1) * LANES)
                accs.append(jnp.dot(p_nb, v_ref[0, pl.ds(k0, win), vs], preferred_element_type=F32)
                            + jnp.dot(p_c, vc_ref[0, :, vs], preferred_element_type=F32))
            o_ref[0, pl.ds(q0, blk_q), :] = _pair_output(accs[0], accs[1]).astype(o_ref.dtype)
        return carry

    lax.fori_loop(0, n_blocks // NA_GROUP, body, 0, unroll=NA_UNROLL)


def _na_bias_table(rpb, grid_rows):
    n_blocks = grid_rows // NA_ROWS
    assert grid_rows % NA_ROWS == 0 and grid_rows >= NA_WIN_ROWS + NA_ROWS
    n_dr, n_dc = 2 * WIN_H - 1, 2 * WIN_W - 1
    ri = np.arange(NA_ROWS)[:, None]
    kj = np.arange(NA_WIN_ROWS)[None, :]
    row_sel, row_valid = [], []
    for blk in (0, 1, n_blocks - 1):
        ws = int(np.clip(blk * NA_ROWS - WIN_H // 2, 0, grid_rows - NA_WIN_ROWS))
        r = blk * NA_ROWS + ri
        krow = ws + kj
        rs = np.clip(r - WIN_H // 2, 0, grid_rows - WIN_H)
        row_valid.append((krow >= rs) & (krow < rs + WIN_H))
        row_sel.append(np.eye(n_dr, dtype=np.float32)[np.clip(krow - r + (WIN_H - 1), 0, n_dr - 1)])
    row_sel = np.stack(row_sel)
    row_valid = np.stack(row_valid)
    cq = np.arange(GRID_W)[:, None]
    ck = np.arange(GRID_W)[None, :]
    cs = np.clip(cq - WIN_W // 2, 0, GRID_W - WIN_W)
    col_valid = (ck >= cs) & (ck < cs + WIN_W)
    col_sel = np.eye(n_dc, dtype=np.float32)[np.clip(ck - cq + (WIN_W - 1), 0, n_dc - 1)]
    rpb2 = rpb.reshape(rpb.shape[0] // 2, 2, n_dr, n_dc) * LOG2E
    t = jnp.einsum("cijd,pade->pcaije", row_sel, rpb2, precision=lax.Precision.HIGHEST)
    t = jnp.einsum("pcaije,qke->pcaiqjk", t, col_sel, precision=lax.Precision.HIGHEST)
    valid = row_valid[:, None, :, None, :, None] & col_valid[None, None, None, :, None, :]
    tab = jnp.where(jnp.asarray(valid)[None], t, NEG_BIG)
    return tab.reshape(tab.shape[:3] + (NA_ROWS * GRID_W, NA_WIN_ROWS * GRID_W))


def _neighbourhood_attention(q, k, v, kc, vc, bias):
    b, s, w = q.shape
    n_pairs = w // LANES
    grid_rows = s // GRID_W
    n_blocks = grid_rows // NA_ROWS
    lc = kc.shape[1]
    return pl.pallas_call(
        functools.partial(_na_kernel, n_blocks=n_blocks, grid_rows=grid_rows),
        grid=(b, n_pairs),
        in_specs=[pl.BlockSpec((1, s, LANES), lambda i, p: (i, 0, p)),
                  pl.BlockSpec((1, s, LANES), lambda i, p: (i, 0, p)),
                  pl.BlockSpec((1, s, 2 * LANES), lambda i, p: (i, 0, p)),
                  pl.BlockSpec((1, lc, LANES), lambda i, p: (i, 0, p)),
                  pl.BlockSpec((1, lc, 2 * LANES), lambda i, p: (i, 0, p)),
                  pl.BlockSpec((1,) + bias.shape[1:], lambda i, p: (p, 0, 0, 0, 0))],
        out_specs=pl.BlockSpec((1, s, LANES), lambda i, p: (i, 0, p)),
        out_shape=jax.ShapeDtypeStruct((b, s, w), BF16),
        compiler_params=_params("parallel", "arbitrary"),
    )(q, k, v, kc, vc, bias)


def _conv_kernel(y_ref, w_ref, vec_ref, o_ref, pad_ref, shift_ref, *, seq):
    ch = y_ref.shape[-1]
    zeros = jnp.zeros((CONV_HALO, ch), F32)
    pad_ref[0:CONV_HALO, :] = zeros
    pad_ref[CONV_HALO + seq:2 * CONV_HALO + seq, :] = zeros
    pad_ref[CONV_HALO:CONV_HALO + seq, :] = y_ref[0]
    first = CONV_HALO - CONV_WIDTH // 2

    span = CONV_TILE + 2 * CONV_HALO - SUBLANES

    def body(i, carry):
        base = pl.multiple_of(i * CONV_TILE, CONV_TILE)
        window = pad_ref[pl.ds(base, CONV_TILE + 2 * CONV_HALO), :]
        for r in range(1, SUBLANES):
            shift_ref[r - 1] = window[r:r + span, :]
        acc = jnp.zeros((CONV_TILE, ch), F32)
        for r in range(SUBLANES):
            for a in range((2 * CONV_HALO) // SUBLANES):
                t = SUBLANES * a + r - first
                if 0 <= t < CONV_WIDTH:
                    rows = slice(SUBLANES * a, SUBLANES * a + CONV_TILE)
                    tap = window[rows, :] if r == 0 else shift_ref[r - 1, rows, :]
                    acc = acc + tap * w_ref[t:t + 1, :]
        acc = acc + vec_ref[0:1, :]
        mu = jnp.mean(acc, axis=-1, keepdims=True)
        cen = acc - mu
        var = jnp.mean(cen * cen, axis=-1, keepdims=True)
        z = cen * lax.rsqrt(var + EPS) * vec_ref[1:2, :] + vec_ref[2:3, :]
        o_ref[0, pl.ds(base, CONV_TILE), :] = _silu(z).astype(o_ref.dtype)
        return carry

    lax.fori_loop(0, seq // CONV_TILE, body, 0)


def _conv_module(y, prm):
    b, l, ch = y.shape
    vec = jnp.stack([prm["conv_dw_b"], prm["conv_ln_g"], prm["conv_ln_b"]] + [jnp.zeros((ch,), F32)] * 5)
    w = jnp.pad(prm["conv_dw_w"], ((0, 1), (0, 0)))
    return pl.pallas_call(
        functools.partial(_conv_kernel, seq=l),
        grid=(b,),
        in_specs=[pl.BlockSpec((1, l, ch), lambda i: (i, 0, 0)), _resident(w.shape), _resident(vec.shape)],
        out_specs=pl.BlockSpec((1, l, ch), lambda i: (i, 0, 0)),
        out_shape=jax.ShapeDtypeStruct((b, l, ch), BF16),
        scratch_shapes=[pltpu.VMEM((l + 2 * CONV_HALO, ch), F32),
                        pltpu.VMEM((SUBLANES - 1, CONV_TILE + 2 * CONV_HALO - SUBLANES, ch), F32)],
        compiler_params=_params("parallel"),
    )(y, w, vec)


def _flat_ctx(a, batch):
    if a.shape[0] == batch:
        return a.reshape(1, batch * a.shape[1], a.shape[2])
    return a.reshape(batch, a.shape[1] // batch, a.shape[2])


def _trunk_layer(xl, xc, mods, prm, even, ctx_out):
    batch = xl.shape[0]
    mod_l, mod_c = mods
    xl = _half_ffn(xl, mod_l[:, 0:3], prm["ffn1_norm"], prm["ffn1_w_in"], prm["ffn1_w_out"])
    xc = _half_ffn(xc, mod_c[:, 0:3], prm["ffn1_norm"], prm["ffn1_w_in"], prm["ffn1_w_out"])
    w_out = prm["w_out"]
    half = w_out.shape[0] // 2
    parts_c = None
    if even:
        weights = _even_weights(prm)
        ql, kl, vl, yl = _even_proj(xl, mod_l[:, 3:5], weights, True)
        qc, kc, vc, yc = _even_proj(xc, mod_c[:, 3:5], weights, False)
        qc, kc, vc, yc = (_flat_ctx(a, batch) for a in (qc, kc, vc, yc))
        parts_l = (_pair_attention(ql, [(kc, vc), (kl, vl)], packed=False), _conv_module(yl, prm))
        w_a, w_b = w_out[:half], w_out[half:]
        if ctx_out:
            parts_c = (_pair_attention(qc, [(kc, vc)], packed=False), _conv_module(yc, prm))
    else:
        weights = _odd_weights(prm)
        cql, ckl, cvl, nql, nkl, nvl = _odd_proj(xl, mod_l[:, 3:5], weights, True)
        ctx_parts = [_flat_ctx(a, batch) for a in _odd_proj(xc, mod_c[:, 3:5], weights, False)]
        cqc, ckc, cvc, nqc, nkc, nvc = ctx_parts
        bias = _na_bias_table(prm["na_rpb"], xl.shape[1] // GRID_W)
        parts_l = (_pair_attention(cql, [(ckc, cvc), (ckl, cvl)], packed=True),
                   _neighbourhood_attention(nql, nkl, nvl, nkc, nvc, bias))
        order = np.concatenate([h * HEAD_DIM + np.arange(HEAD_DIM) for h in GQA_ORDER])
        w_a, w_b = w_out[:half][order], w_out[half:]
        if ctx_out:
            parts_c = (_pair_attention(cqc, [(ckc, cvc)], packed=True), _na_ctx_attention(nqc, nkc, nvc))

    def mixer_and_ffn2(x, mod, parts):
        mod4 = jnp.concatenate([mod[:, 6:9], mod[:, 5:6]], axis=1)
        return _half_ffn(x, mod4, prm["ffn2_norm"], prm["ffn2_w_in"], prm["ffn2_w_out"], mixer=parts + (w_a, w_b))

    xl = mixer_and_ffn2(xl, mod_l, parts_l)
    if ctx_out:
        xc = mixer_and_ffn2(xc, mod_c, tuple(_flat_ctx(p, batch) for p in parts_c))
    return xl, xc


def _layer_mods(c, c_ctx, prm):
    batch, d = c.shape
    rows = -(-(batch + 1) // 8) * 8
    cond = jnp.zeros((rows, d), F32).at[:batch].set(c).at[batch].set(c_ctx)
    mod = _modulation(cond, prm["mod_w"], prm["mod_b"]).reshape(rows, N_MOD, d)
    return mod[:batch], mod[batch:batch + 1]


def kernel(x, c, ctx, c_ctx,
           l0_mod_w, l0_mod_b, l0_ffn1_norm, l0_ffn1_w_in, l0_ffn1_w_out, l0_mix_norm, l0_w_in,
           l0_mla_q_norm, l0_mla_w_uq, l0_mla_kv_norm, l0_mla_w_ukv, l0_mla_q_gain, l0_mla_k_gain,
           l0_conv_glu_b, l0_conv_dw_w, l0_conv_dw_b, l0_conv_ln_g, l0_conv_ln_b,
           l0_w_out, l0_ffn2_norm, l0_ffn2_w_in, l0_ffn2_w_out,
           l1_mod_w, l1_mod_b, l1_ffn1_norm, l1_ffn1_w_in, l1_ffn1_w_out, l1_mix_norm, l1_w_in,
           l1_gqa_q_gain, l1_gqa_k_gain, l1_na_q_gain, l1_na_k_gain, l1_na_rpb,
           l1_w_out, l1_ffn2_norm, l1_ffn2_w_in, l1_ffn2_w_out):
    layers = (
        dict(mod_w=l0_mod_w, mod_b=l0_mod_b, ffn1_norm=l0_ffn1_norm, ffn1_w_in=l0_ffn1_w_in,
             ffn1_w_out=l0_ffn1_w_out, mix_norm=l0_mix_norm, w_in=l0_w_in,
             mla_q_norm=l0_mla_q_norm, mla_w_uq=l0_mla_w_uq, mla_kv_norm=l0_mla_kv_norm,
             mla_w_ukv=l0_mla_w_ukv, mla_q_gain=l0_mla_q_gain, mla_k_gain=l0_mla_k_gain,
             conv_glu_b=l0_conv_glu_b, conv_dw_w=l0_conv_dw_w, conv_dw_b=l0_conv_dw_b,
             conv_ln_g=l0_conv_ln_g, conv_ln_b=l0_conv_ln_b, w_out=l0_w_out,
             ffn2_norm=l0_ffn2_norm, ffn2_w_in=l0_ffn2_w_in, ffn2_w_out=l0_ffn2_w_out),
        dict(mod_w=l1_mod_w, mod_b=l1_mod_b, ffn1_norm=l1_ffn1_norm, ffn1_w_in=l1_ffn1_w_in,
             ffn1_w_out=l1_ffn1_w_out, mix_norm=l1_mix_norm, w_in=l1_w_in,
             gqa_q_gain=l1_gqa_q_gain, gqa_k_gain=l1_gqa_k_gain, na_q_gain=l1_na_q_gain,
             na_k_gain=l1_na_k_gain, na_rpb=l1_na_rpb, w_out=l1_w_out,
             ffn2_norm=l1_ffn2_norm, ffn2_w_in=l1_ffn2_w_in, ffn2_w_out=l1_ffn2_w_out),
    )
    batch = x.shape[0]
    xl, xc = x, _flat_ctx(ctx, batch)
    for i, prm in enumerate(layers):
        mods = _layer_mods(c, c_ctx, prm)
        xl, xc = _trunk_layer(xl, xc, mods, prm, even=(i % 2 == 0), ctx_out=(i < len(layers) - 1))
    return xl
```

```python
import functools
import math

import numpy as np
import jax
import jax.numpy as jnp
from jax import lax
from jax.experimental import pallas as pl
from jax.experimental.pallas import tpu as pltpu

F32 = jnp.float32
BF16 = jnp.bfloat16

D_MODEL = 1024
CTX_LEN = 256
GRID_W = 64
D_FF = 2816
N_MOD = 9
EPS = 1e-6
ROPE_BASE = 10000.0
MLA_HEADS = 8
MLA_NOPE = 64
MLA_ROPE = 32
MLA_V = 64
MLA_Q_LORA = 384
MLA_KV_LORA = 256
CONV_CH = 512
CONV_WIDTH = 31
HEAD_DIM = 64
GQA_HEADS = 8
GQA_KV_HEADS = 2
NA_HEADS = 8
WIN_H = 8
WIN_W = 16

LANES = 128
SUBLANES = 8
MXU_DIM = 256
VMEM_LIMIT = 56 * 1024 * 1024
LOG2E = math.log2(math.e)
NEG_BIG = -1e30

FFN_CHUNK = 256
TOKEN_TILE = 512
FFN_TILE = 1024
ATT_TQ = 1024
ATT_SUB = 512
ATT_TK = 256
NA_ROWS = 4
NA_WIN_ROWS = 12
NA_GROUP = 4
NA_UNROLL = 2
CONV_TILE = 256
CONV_HALO = 16


def _params(*sem):
    return pltpu.CompilerParams(dimension_semantics=sem, vmem_limit_bytes=VMEM_LIMIT)


def _resident(shape):
    return pl.BlockSpec(shape, lambda *_: (0,) * len(shape), pipeline_mode=pl.Buffered(1))


def _silu(x):
    return x * jax.nn.sigmoid(x)


def _rms_rows(x):
    return x * lax.rsqrt(jnp.mean(x * x, axis=-1, keepdims=True) + EPS)


def _seg_sumsq(x, g):
    outs = []
    for p in range(x.shape[-1] // MXU_DIM):
        sq = x[:, p * MXU_DIM:(p + 1) * MXU_DIM]
        outs.append(jnp.dot((sq * sq).astype(BF16), g, preferred_element_type=F32))
    return outs[0] if len(outs) == 1 else jnp.concatenate(outs, axis=-1)


def _swap_halves(x, half, lo):
    lane = lax.broadcasted_iota(jnp.int32, x.shape, 1)
    first = ((lane - lo) & (2 * half - 1)) < half
    return jnp.where(first, pltpu.roll(x, LANES - half, 1), pltpu.roll(x, half, 1))


def _mod_kernel(c_ref, w_ref, b_ref, o_ref):
    a = _silu(c_ref[...]).astype(BF16)
    o_ref[...] = jnp.dot(a, w_ref[...].astype(BF16), preferred_element_type=F32) + b_ref[...]


def _modulation(cond, mod_w, mod_b):
    rows, d = cond.shape
    n = mod_w.shape[1]
    return pl.pallas_call(
        _mod_kernel,
        grid=(n // d,),
        in_specs=[pl.BlockSpec((rows, d), lambda j: (0, 0)),
                  pl.BlockSpec((d, d), lambda j: (0, j)),
                  pl.BlockSpec((1, d), lambda j: (0, j))],
        out_specs=pl.BlockSpec((rows, d), lambda j: (0, j)),
        out_shape=jax.ShapeDtypeStruct((rows, n), F32),
        compiler_params=_params("arbitrary"),
    )(cond, mod_w, mod_b.reshape(1, n))


def _ffn_kernel(*refs, mixer):
    if mixer:
        x_ref, mod_ref, g_ref, win_ref, wout_ref, pa_ref, pb_ref, wa_ref, wb_ref, o_ref, h_ref, a_ref = refs
    else:
        x_ref, mod_ref, g_ref, win_ref, wout_ref, o_ref, h_ref, a_ref = refs
    x = x_ref[0]
    mod = mod_ref[0]
    if mixer:
        mix = (jnp.dot(pa_ref[0], wa_ref[...], preferred_element_type=F32)
               + jnp.dot(pb_ref[0], wb_ref[...], preferred_element_type=F32))
        x = x + mod[3:4] * mix
    h = _rms_rows(x) * g_ref[...]
    h_ref[...] = (h * (1.0 + mod[1:2]) + mod[0:1]).astype(BF16)
    for c in range(D_FF // FFN_CHUNK):
        gate = jnp.dot(h_ref[...], win_ref[:, c * FFN_CHUNK:(c + 1) * FFN_CHUNK], preferred_element_type=F32)
        up = jnp.dot(h_ref[...], win_ref[:, D_FF + c * FFN_CHUNK:D_FF + (c + 1) * FFN_CHUNK], preferred_element_type=F32)
        a_ref[:, c * FFN_CHUNK:(c + 1) * FFN_CHUNK] = (_silu(gate) * up).astype(BF16)
    y = jnp.dot(a_ref[...], wout_ref[...], preferred_element_type=F32)
    o_ref[0] = x + (0.5 * mod[2:3]) * y


def _half_ffn(x, mod, norm_g, w_in, w_out, mixer=None):
    b, l, d = x.shape
    tm = min(FFN_TILE, l)
    tok = lambda w: pl.BlockSpec((1, tm, w), lambda i, j: (i, j, 0))
    in_specs = [tok(d), pl.BlockSpec((1,) + mod.shape[1:], lambda i, j: (i, 0, 0)),
                _resident((1, d)), _resident(w_in.shape), _resident(w_out.shape)]
    args = [x, mod, norm_g.reshape(1, d), w_in.astype(BF16), w_out.astype(BF16)]
    if mixer is not None:
        part_a, part_b, w_a, w_b = mixer
        in_specs += [tok(part_a.shape[-1]), tok(part_b.shape[-1]), _resident(w_a.shape), _resident(w_b.shape)]
        args += [part_a, part_b, w_a.astype(BF16), w_b.astype(BF16)]
    return pl.pallas_call(
        functools.partial(_ffn_kernel, mixer=mixer is not None),
        grid=(b, l // tm),
        in_specs=in_specs,
        out_specs=tok(d),
        out_shape=jax.ShapeDtypeStruct((b, l, d), F32),
        scratch_shapes=[pltpu.VMEM((tm, d), BF16), pltpu.VMEM((tm, D_FF), BF16)],
        compiler_params=_params("parallel", "parallel"),
    )(*args)


E_KR = MLA_KV_LORA
E_CQ = E_KR + LANES
E_CA = E_CQ + MLA_Q_LORA
E_CG = E_CA + CONV_CH
E_COLS = E_CG + CONV_CH
HEADS_W = MLA_HEADS * LANES


def _even_proj_kernel(x_ref, mod_ref, vec_ref, win_ref, wuk_ref, wuv_ref, wuq_ref, g_ref, c_ref, s_ref,
                      q_ref, k_ref, v_ref, y_ref):
    x = x_ref[0]
    mod = mod_ref[0]
    h = _rms_rows(x) * vec_ref[0:1, :]
    h = (h * (1.0 + mod[1:2]) + mod[0:1]).astype(BF16)
    pp = jnp.dot(h, win_ref[...], preferred_element_type=F32)
    cos = c_ref[...]
    sin = s_ref[...]
    g = g_ref[...]

    cq = (_rms_rows(pp[:, E_CQ:E_CA]) * vec_ref[2:3, 0:MLA_Q_LORA]).astype(BF16)
    q = jnp.dot(cq, wuq_ref[...], preferred_element_type=F32)
    q = q * lax.rsqrt(_seg_sumsq(q, g) * vec_ref[6:7, :] + EPS) * vec_ref[5:6, :]
    for hd in range(MLA_HEADS):
        sl = slice(hd * LANES, (hd + 1) * LANES)
        qh = q[:, sl]
        qh = qh * cos + _swap_halves(qh, MLA_ROPE // 2, MLA_NOPE) * sin
        q_ref[0, :, sl] = qh.astype(BF16)

    ckv = (_rms_rows(pp[:, 0:E_KR]) * vec_ref[1:2, 0:MLA_KV_LORA]).astype(BF16)
    kn = jnp.dot(ckv, wuk_ref[...], preferred_element_type=F32)
    vv = jnp.dot(ckv, wuv_ref[...], preferred_element_type=F32)
    kn = kn * lax.rsqrt(_seg_sumsq(kn, g) * vec_ref[6:7, :] + EPS) * vec_ref[3:4, :]
    kr = pp[:, E_KR:E_CQ]
    kr = kr * lax.rsqrt(jnp.sum(kr * kr, axis=-1, keepdims=True) * (1.0 / MLA_ROPE) + EPS) * vec_ref[4:5, 0:LANES]
    kr = kr * cos + _swap_halves(kr, MLA_ROPE // 2, MLA_NOPE) * sin
    v_ref[0] = (vv + vec_ref[8:9, :]).astype(BF16)
    for hd in range(MLA_HEADS):
        sl = slice(hd * LANES, (hd + 1) * LANES)
        k_ref[0, :, sl] = (kn[:, sl] + kr).astype(BF16)

    glu_b = vec_ref[7:8, :]
    y_ref[0] = (pp[:, E_CA:E_CG] + glu_b[:, 0:CONV_CH]) * jax.nn.sigmoid(pp[:, E_CG:E_COLS] + glu_b[:, CONV_CH:])


def _even_weights(prm):
    d = D_MODEL
    ev = np.arange(0, MLA_ROPE, 2)
    od = np.arange(1, MLA_ROPE, 2)
    zero_col = prm["w_in"].shape[1]
    cols = np.concatenate([
        np.arange(MLA_KV_LORA),
        np.full(MLA_NOPE, zero_col), MLA_KV_LORA + ev, MLA_KV_LORA + od, np.full(LANES - MLA_NOPE - MLA_ROPE, zero_col),
        np.arange(MLA_KV_LORA + MLA_ROPE, zero_col)])
    w_ext = jnp.concatenate([prm["w_in"], jnp.zeros((d, 1), F32)], axis=1)
    win = jnp.take(w_ext, cols, axis=1).astype(BF16)

    qd = MLA_NOPE + MLA_ROPE
    zq = MLA_HEADS * qd
    qcols = np.concatenate([np.concatenate([h * qd + np.arange(MLA_NOPE), h * qd + MLA_NOPE + ev, h * qd + MLA_NOPE + od,
                                            np.full(LANES - qd, zq)]) for h in range(MLA_HEADS)])
    wuq = jnp.take(jnp.concatenate([prm["mla_w_uq"], jnp.zeros((MLA_Q_LORA, 1), F32)], axis=1), qcols, axis=1).astype(BF16)
    kvd = MLA_NOPE + MLA_V
    zk = MLA_HEADS * kvd
    kcols = np.concatenate([np.concatenate([h * kvd + np.arange(MLA_NOPE), np.full(LANES - MLA_NOPE, zk)])
                            for h in range(MLA_HEADS)])
    vcols = np.concatenate([np.concatenate([h * kvd + MLA_NOPE + np.arange(MLA_V), np.full(LANES - MLA_V, zk)])
                            for h in range(MLA_HEADS)])
    wukv = jnp.concatenate([prm["mla_w_ukv"], jnp.zeros((MLA_KV_LORA, 1), F32)], axis=1)
    wuk = jnp.take(wukv, kcols, axis=1).astype(BF16)
    wuv = jnp.take(wukv, vcols, axis=1).astype(BF16)

    def row(v):
        return jnp.pad(v.astype(F32), (0, HEADS_W - v.shape[0]))

    rope_perm = np.concatenate([MLA_NOPE + ev, MLA_NOPE + od])
    pad_q = jnp.zeros((LANES - qd,), F32)
    kg, qg = prm["mla_k_gain"], prm["mla_q_gain"]
    q_scale = (qd ** -0.5) * LOG2E
    inv_cnt = np.tile(np.concatenate([np.full(MLA_NOPE, 1.0 / MLA_NOPE), np.full(MLA_ROPE, 1.0 / MLA_ROPE),
                                      np.ones(LANES - qd)]), MLA_HEADS).astype(np.float32)
    ones_lane = np.tile((np.arange(LANES) == MLA_V).astype(np.float32), MLA_HEADS)
    vec = jnp.stack([
        row(prm["mix_norm"]),
        row(prm["mla_kv_norm"]),
        row(prm["mla_q_norm"]),
        jnp.tile(jnp.concatenate([kg[:MLA_NOPE], jnp.zeros((LANES - MLA_NOPE,), F32)]), MLA_HEADS),
        row(jnp.concatenate([jnp.zeros((MLA_NOPE,), F32), kg[rope_perm], pad_q])),
        jnp.tile(jnp.concatenate([qg[:MLA_NOPE], qg[rope_perm], pad_q]), MLA_HEADS) * q_scale,
        jnp.asarray(inv_cnt),
        row(prm["conv_glu_b"]),
        jnp.asarray(ones_lane),
    ] + [jnp.zeros((HEADS_W,), F32)] * 7)

    seg = np.arange(MXU_DIM)
    seg_id = (seg // LANES) * 4 + np.where(seg % LANES < MLA_NOPE, 0, np.where(seg % LANES < qd, 1, 2))
    gmat = jnp.asarray((seg_id[:, None] == seg_id[None, :]).astype(np.float32)).astype(BF16)
    return win, wuk, wuv, wuq, vec, gmat


def _mla_rope_tables(n_tokens, use_rope):
    ones = jnp.ones((n_tokens, MLA_NOPE), F32)
    tail = jnp.ones((n_tokens, LANES - MLA_NOPE - MLA_ROPE), F32)
    if not use_rope:
        c = jnp.ones((n_tokens, LANES), F32)
        return c, jnp.zeros_like(c)
    cos, sin = _axial_angles(n_tokens, MLA_ROPE)
    c = jnp.concatenate([ones, cos, cos, tail], axis=-1)
    s = jnp.concatenate([0 * ones, -sin, sin, 0 * tail], axis=-1)
    return c, s


def _axial_angles(n_tokens, dim):
    t = jnp.arange(n_tokens)
    row = (t // GRID_W).astype(F32)
    col = (t % GRID_W).astype(F32)
    n_pairs = dim // 4
    inv = ROPE_BASE ** (-jnp.arange(n_pairs, dtype=F32) / n_pairs)
    ang = jnp.concatenate([row[:, None] * inv, col[:, None] * inv], axis=-1)
    return jnp.cos(ang), jnp.sin(ang)


def _even_proj(x, mod2, weights, use_rope):
    win, wuk, wuv, wuq, vec, gmat = weights
    b, l, d = x.shape
    tm = min(TOKEN_TILE, l)
    cos, sin = _mla_rope_tables(l, use_rope)
    tok = lambda w: pl.BlockSpec((1, tm, w), lambda i, j: (i, j, 0))
    return pl.pallas_call(
        _even_proj_kernel,
        grid=(b, l // tm),
        in_specs=[tok(d),
                  pl.BlockSpec((1, 2, d), lambda i, j: (i, 0, 0)),
                  _resident(vec.shape), _resident(win.shape), _resident(wuk.shape), _resident(wuv.shape),
                  _resident(wuq.shape), _resident(gmat.shape),
                  pl.BlockSpec((tm, LANES), lambda i, j: (j, 0)),
                  pl.BlockSpec((tm, LANES), lambda i, j: (j, 0))],
        out_specs=[tok(HEADS_W), tok(HEADS_W), tok(HEADS_W), tok(CONV_CH)],
        out_shape=[jax.ShapeDtypeStruct((b, l, HEADS_W), BF16)] * 3 + [jax.ShapeDtypeStruct((b, l, CONV_CH), F32)],
        compiler_params=_params("parallel", "parallel"),
    )(x, mod2, vec, win, wuk, wuv, wuq, gmat, cos, sin)


O_CK = 0
O_CV = O_CK + GQA_KV_HEADS * HEAD_DIM
O_NK = O_CV + GQA_KV_HEADS * LANES
O_NV = O_NK + NA_HEADS * HEAD_DIM
O_CQ = O_NV + NA_HEADS * LANES
O_NQ = O_CQ + GQA_HEADS * HEAD_DIM
O_COLS = O_NQ + NA_HEADS * HEAD_DIM
GQA_ORDER = (0, 4, 1, 5, 2, 6, 3, 7)


def _rope64(x, cos, sin):
    outs = []
    for p in range(x.shape[-1] // LANES):
        xs = x[:, p * LANES:(p + 1) * LANES]
        outs.append(xs * cos + _swap_halves(xs, HEAD_DIM // 2, 0) * sin)
    return outs[0] if len(outs) == 1 else jnp.concatenate(outs, axis=-1)


def _odd_proj_kernel(x_ref, mod_ref, vec_ref, win_ref, g_ref, c_ref, s_ref,
                     cq_ref, ck_ref, cv_ref, nq_ref, nk_ref, nv_ref):
    x = x_ref[0]
    mod = mod_ref[0]
    h = _rms_rows(x) * vec_ref[0:1, :]
    h = (h * (1.0 + mod[1:2]) + mod[0:1]).astype(BF16)
    pp = jnp.dot(h, win_ref[...], preferred_element_type=F32)
    cos = c_ref[...]
    sin = s_ref[...]
    g = g_ref[...]
    inv = 1.0 / HEAD_DIM

    def head_norm(v, gain):
        if v.shape[-1] == LANES:
            ss = jnp.dot((v * v).astype(BF16), g[0:LANES, 0:LANES], preferred_element_type=F32)
        else:
            ss = _seg_sumsq(v, g)
        return v * lax.rsqrt(ss * inv + EPS) * gain

    wq = GQA_HEADS * HEAD_DIM
    cq = head_norm(pp[:, O_CQ:O_NQ], vec_ref[3:4, 0:wq])
    cq_ref[0] = _rope64(cq, cos, sin).astype(BF16)
    ck = head_norm(pp[:, O_CK:O_CV], vec_ref[1:2, 0:LANES])
    ck_ref[0] = _rope64(ck, cos, sin).astype(BF16)
    nq_ref[0] = head_norm(pp[:, O_NQ:O_COLS], vec_ref[4:5, 0:wq]).astype(BF16)
    nk_ref[0] = head_norm(pp[:, O_NK:O_NV], vec_ref[2:3, 0:wq]).astype(BF16)
    cv_ref[0] = (pp[:, O_CV:O_NK] + vec_ref[5:6, 0:GQA_KV_HEADS * LANES]).astype(BF16)
    nv_ref[0] = (pp[:, O_NV:O_CQ] + vec_ref[5:6, :]).astype(BF16)


def _odd_weights(prm):
    d = D_MODEL
    ev = np.arange(0, HEAD_DIM, 2)
    od = np.arange(1, HEAD_DIM, 2)
    kvw = GQA_KV_HEADS * HEAD_DIM
    naw = NA_HEADS * HEAD_DIM
    zero_col = prm["w_in"].shape[1]
    pad64 = np.full(LANES - HEAD_DIM, zero_col)
    q0 = 2 * kvw + 2 * naw
    cols = np.concatenate(
        [np.concatenate([g * HEAD_DIM + ev, g * HEAD_DIM + od]) for g in range(GQA_KV_HEADS)]
        + [np.concatenate([kvw + g * HEAD_DIM + np.arange(HEAD_DIM), pad64]) for g in range(GQA_KV_HEADS)]
        + [2 * kvw + np.arange(naw)]
        + [np.concatenate([2 * kvw + naw + h * HEAD_DIM + np.arange(HEAD_DIM), pad64]) for h in range(NA_HEADS)]
        + [np.concatenate([q0 + h * HEAD_DIM + ev, q0 + h * HEAD_DIM + od]) for h in GQA_ORDER]
        + [q0 + GQA_HEADS * HEAD_DIM + np.arange(naw)])
    assert cols.shape[0] == O_COLS
    w_ext = jnp.concatenate([prm["w_in"], jnp.zeros((d, 1), F32)], axis=1)
    win = jnp.take(w_ext, cols, axis=1).astype(BF16)

    def row(v):
        return jnp.pad(v.astype(F32), (0, HEADS_W - v.shape[0]))

    perm = np.concatenate([ev, od])
    q_scale = (HEAD_DIM ** -0.5) * LOG2E
    ones_lane = np.tile((np.arange(LANES) == HEAD_DIM).astype(np.float32), NA_HEADS)
    vec = jnp.stack([
        row(prm["mix_norm"]),
        row(jnp.tile(prm["gqa_k_gain"][perm], GQA_KV_HEADS)),
        row(jnp.tile(prm["na_k_gain"], NA_HEADS)),
        row(jnp.tile(prm["gqa_q_gain"][perm], GQA_HEADS) * q_scale),
        row(jnp.tile(prm["na_q_gain"], NA_HEADS) * q_scale),
        jnp.asarray(ones_lane),
    ] + [jnp.zeros((HEADS_W,), F32)] * 2)
    seg_id = np.arange(MXU_DIM) // HEAD_DIM
    gmat = jnp.asarray((seg_id[:, None] == seg_id[None, :]).astype(np.float32)).astype(BF16)
    return win, vec, gmat


def _gqa_rope_tables(n_tokens, use_rope):
    if not use_rope:
        c = jnp.ones((n_tokens, LANES), F32)
        return c, jnp.zeros_like(c)
    cos, sin = _axial_angles(n_tokens, HEAD_DIM)
    return jnp.concatenate([cos, cos, cos, cos], axis=-1), jnp.concatenate([-sin, sin, -sin, sin], axis=-1)


def _odd_proj(x, mod2, weights, use_rope):
    win, vec, gmat = weights
    b, l, d = x.shape
    tm = min(TOKEN_TILE, l)
    cos, sin = _gqa_rope_tables(l, use_rope)
    tok = lambda w: pl.BlockSpec((1, tm, w), lambda i, j: (i, j, 0))
    widths = (GQA_HEADS * HEAD_DIM, GQA_KV_HEADS * HEAD_DIM, GQA_KV_HEADS * LANES,
              NA_HEADS * HEAD_DIM, NA_HEADS * HEAD_DIM, NA_HEADS * LANES)
    return pl.pallas_call(
        _odd_proj_kernel,
        grid=(b, l // tm),
        in_specs=[tok(d),
                  pl.BlockSpec((1, 2, d), lambda i, j: (i, 0, 0)),
                  _resident(vec.shape), _resident(win.shape), _resident(gmat.shape),
                  pl.BlockSpec((tm, LANES), lambda i, j: (j, 0)),
                  pl.BlockSpec((tm, LANES), lambda i, j: (j, 0))],
        out_specs=[tok(w) for w in widths],
        out_shape=[jax.ShapeDtypeStruct((b, l, w), BF16) for w in widths],
        compiler_params=_params("parallel", "parallel"),
    )(x, mod2, vec, win, gmat, cos, sin)


def _pair_queries(q, packed):
    if not packed:
        return q[:, 0:LANES], q[:, LANES:2 * LANES]
    lane = lax.broadcasted_iota(jnp.int32, q.shape, 1)
    zero = jnp.zeros_like(q)
    return jnp.where(lane < HEAD_DIM, q, zero), jnp.where(lane >= HEAD_DIM, q, zero)


def _pair_output(acc_a, acc_b):
    oa = acc_a / acc_a[:, HEAD_DIM:HEAD_DIM + 1]
    ob = acc_b / acc_b[:, HEAD_DIM:HEAD_DIM + 1]
    lane = lax.broadcasted_iota(jnp.int32, oa.shape, 1)
    return jnp.where(lane < HEAD_DIM, oa, pltpu.roll(ob, HEAD_DIM, 1))


def _score(q, k):
    return lax.dot_general(q, k, (((1,), (1,)), ((), ())), preferred_element_type=F32)


def _attn_kernel(*refs, n_src, packed, src_len):
    q_ref = refs[0]
    kv_refs = refs[1:1 + 2 * n_src]
    o_ref = refs[1 + 2 * n_src]
    rows = min(ATT_SUB, q_ref.shape[1])
    for r0 in range(0, q_ref.shape[1], rows):
        qs = _pair_queries(q_ref[0, r0:r0 + rows, :], packed)
        m = [jnp.full((rows, 1), NEG_BIG, F32) for _ in range(2)]
        acc = [jnp.zeros((rows, LANES), F32) for _ in range(2)]
        for s in range(n_src):
            k_ref, v_ref = kv_refs[2 * s], kv_refs[2 * s + 1]
            tk = min(ATT_TK, src_len[s])
            for blk in range(src_len[s] // tk):
                rs = slice(blk * tk, (blk + 1) * tk)
                for hd in range(2):
                    ks = slice(0, LANES) if packed else slice(hd * LANES, (hd + 1) * LANES)
                    sc = _score(qs[hd], k_ref[0, rs, ks])
                    m_new = jnp.maximum(m[hd], jnp.max(sc, axis=-1, keepdims=True))
                    p = jnp.exp2(sc - m_new).astype(BF16)
                    pv = jnp.dot(p, v_ref[0, rs, hd * LANES:(hd + 1) * LANES], preferred_element_type=F32)
                    acc[hd] = acc[hd] * jnp.exp2(m[hd] - m_new) + pv
                    m[hd] = m_new
        o_ref[0, r0:r0 + rows, :] = _pair_output(acc[0], acc[1]).astype(o_ref.dtype)


def _pair_attention(q, sources, packed):
    b, lq, qw = q.shape
    wq = LANES if packed else 2 * LANES
    n_pairs = qw // wq
    tq = min(ATT_TQ, lq)
    in_specs = [pl.BlockSpec((1, tq, wq), lambda i, p, j: (i, j, p))]
    args = [q]
    for k, v in sources:
        lk = k.shape[1]
        if packed:
            in_specs.append(pl.BlockSpec((1, lk, LANES), lambda i, p, j: (i, 0, 0)))
            in_specs.append(pl.BlockSpec((1, lk, 2 * LANES), lambda i, p, j: (i, 0, 0)))
        else:
            in_specs.append(pl.BlockSpec((1, lk, 2 * LANES), lambda i, p, j: (i, 0, p)))
            in_specs.append(pl.BlockSpec((1, lk, 2 * LANES), lambda i, p, j: (i, 0, p)))
        args += [k, v]
    kern = functools.partial(_attn_kernel, n_src=len(sources), packed=packed,
                             src_len=tuple(k.shape[1] for k, _ in sources))
    return pl.pallas_call(
        kern,
        grid=(b, n_pairs, lq // tq),
        in_specs=in_specs,
        out_specs=pl.BlockSpec((1, tq, LANES), lambda i, p, j: (i, j, p)),
        out_shape=jax.ShapeDtypeStruct((b, lq, n_pairs * LANES), BF16),
        compiler_params=_params("parallel", "parallel", "arbitrary"),
    )(*args)


def _na_ctx_kernel(q_ref, k_ref, v_ref, o_ref):
    qs = _pair_queries(q_ref[0], True)
    accs = []
    for hd in range(2):
        sc = _score(qs[hd], k_ref[0])
        p = jnp.exp2(sc - jnp.max(sc, axis=-1, keepdims=True)).astype(BF16)
        accs.append(jnp.dot(p, v_ref[0, :, hd * LANES:(hd + 1) * LANES], preferred_element_type=F32))
    o_ref[0] = _pair_output(accs[0], accs[1]).astype(o_ref.dtype)


def _na_ctx_attention(q, k, v):
    b, l, w = q.shape
    n_pairs = w // LANES
    return pl.pallas_call(
        _na_ctx_kernel,
        grid=(b, n_pairs),
        in_specs=[pl.BlockSpec((1, l, LANES), lambda i, p: (i, 0, p)),
                  pl.BlockSpec((1, l, LANES), lambda i, p: (i, 0, p)),
                  pl.BlockSpec((1, l, 2 * LANES), lambda i, p: (i, 0, p))],
        out_specs=pl.BlockSpec((1, l, LANES), lambda i, p: (i, 0, p)),
        out_shape=jax.ShapeDtypeStruct((b, l, w), BF16),
        compiler_params=_params("parallel", "parallel"),
    )(q, k, v)


def _na_kernel(q_ref, k_ref, v_ref, kc_ref, vc_ref, bias_ref, o_ref, *, n_blocks, grid_rows):
    blk_q = NA_ROWS * GRID_W
    win = NA_WIN_ROWS * GRID_W
    kc = kc_ref[0]

    def body(g, carry):
        work = []
        for u in range(NA_GROUP):
            i = g * NA_GROUP + u
            q0 = pl.multiple_of(i * blk_q, blk_q)
            ws = jnp.clip(i * NA_ROWS - WIN_H // 2, 0, grid_rows - NA_WIN_ROWS)
            k0 = pl.multiple_of(ws * GRID_W, GRID_W)
            cfg = jnp.where(i == 0, 0, jnp.where(i == n_blocks - 1, 2, 1))
            qs = _pair_queries(q_ref[0, pl.ds(q0, blk_q), :], True)
            kw = k_ref[0, pl.ds(k0, win), :]
            sc = [(_score(qs[hd], kw) + bias_ref[0, cfg, hd], _score(qs[hd], kc)) for hd in range(2)]
            work.append((q0, k0, sc))
        for q0, k0, sc in work:
            accs = []
            for hd in range(2):
                s_nb, s_c = sc[hd]
                m = jnp.maximum(jnp.max(s_nb, axis=-1, keepdims=True), jnp.max(s_c, axis=-1, keepdims=True))
                p_nb = jnp.exp2(s_nb - m).astype(BF16)
                p_c = jnp.exp2(s_c - m).astype(BF16)
                vs = slice(hd * LANES, (hd + 1) * LANES)
                accs.append(jnp.dot(p_nb, v_ref[0, pl.ds(k0, win), vs], preferred_element_type=F32)
                            + jnp.dot(p_c, vc_ref[0, :, vs], preferred_element_type=F32))
            o_ref[0, pl.ds(q0, blk_q), :] = _pair_output(accs[0], accs[1]).astype(o_ref.dtype)
        return carry

    lax.fori_loop(0, n_blocks // NA_GROUP, body, 0, unroll=NA_UNROLL)


def _na_bias_table(rpb, grid_rows):
    n_blocks = grid_rows // NA_ROWS
    assert grid_rows % NA_ROWS == 0 and grid_rows >= NA_WIN_ROWS + NA_ROWS
    n_dr, n_dc = 2 * WIN_H - 1, 2 * WIN_W - 1
    ri = np.arange(NA_ROWS)[:, None]
    kj = np.arange(NA_WIN_ROWS)[None, :]
    row_sel, row_valid = [], []
    for blk in (0, 1, n_blocks - 1):
        ws = int(np.clip(blk * NA_ROWS - WIN_H // 2, 0, grid_rows - NA_WIN_ROWS))
        r = blk * NA_ROWS + ri
        krow = ws + kj
        rs = np.clip(r - WIN_H // 2, 0, grid_rows - WIN_H)
        row_valid.append((krow >= rs) & (krow < rs + WIN_H))
        row_sel.append(np.eye(n_dr, dtype=np.float32)[np.clip(krow - r + (WIN_H - 1), 0, n_dr - 1)])
    row_sel = np.stack(row_sel)
    row_valid = np.stack(row_valid)
    cq = np.arange(GRID_W)[:, None]
    ck = np.arange(GRID_W)[None, :]
    cs = np.clip(cq - WIN_W // 2, 0, GRID_W - WIN_W)
    col_valid = (ck >= cs) & (ck < cs + WIN_W)
    col_sel = np.eye(n_dc, dtype=np.float32)[np.clip(ck - cq + (WIN_W - 1), 0, n_dc - 1)]
    rpb2 = rpb.reshape(rpb.shape[0] // 2, 2, n_dr, n_dc) * LOG2E
    t = jnp.einsum("cijd,pade->pcaije", row_sel, rpb2, precision=lax.Precision.HIGHEST)
    t = jnp.einsum("pcaije,qke->pcaiqjk", t, col_sel, precision=lax.Precision.HIGHEST)
    valid = row_valid[:, None, :, None, :, None] & col_valid[None, None, None, :, None, :]
    tab = jnp.where(jnp.asarray(valid)[None], t, NEG_BIG)
    return tab.reshape(tab.shape[:3] + (NA_ROWS * GRID_W, NA_WIN_ROWS * GRID_W))


def _neighbourhood_attention(q, k, v, kc, vc, bias):
    b, s, w = q.shape
    n_pairs = w // LANES
    grid_rows = s // GRID_W
    n_blocks = grid_rows // NA_ROWS
    lc = kc.shape[1]
    return pl.pallas_call(
        functools.partial(_na_kernel, n_blocks=n_blocks, grid_rows=grid_rows),
        grid=(b, n_pairs),
        in_specs=[pl.BlockSpec((1, s, LANES), lambda i, p: (i, 0, p)),
                  pl.BlockSpec((1, s, LANES), lambda i, p: (i, 0, p)),
                  pl.BlockSpec((1, s, 2 * LANES), lambda i, p: (i, 0, p)),
                  pl.BlockSpec((1, lc, LANES), lambda i, p: (i, 0, p)),
                  pl.BlockSpec((1, lc, 2 * LANES), lambda i, p: (i, 0, p)),
                  pl.BlockSpec((1,) + bias.shape[1:], lambda i, p: (p, 0, 0, 0, 0))],
        out_specs=pl.BlockSpec((1, s, LANES), lambda i, p: (i, 0, p)),
        out_shape=jax.ShapeDtypeStruct((b, s, w), BF16),
        compiler_params=_params("parallel", "arbitrary"),
    )(q, k, v, kc, vc, bias)


def _conv_kernel(y_ref, w_ref, vec_ref, o_ref, pad_ref, shift_ref, *, seq):
    ch = y_ref.shape[-1]
    zeros = jnp.zeros((CONV_HALO, ch), F32)
    pad_ref[0:CONV_HALO, :] = zeros
    pad_ref[CONV_HALO + seq:2 * CONV_HALO + seq, :] = zeros
    pad_ref[CONV_HALO:CONV_HALO + seq, :] = y_ref[0]
    first = CONV_HALO - CONV_WIDTH // 2

    span = CONV_TILE + 2 * CONV_HALO - SUBLANES

    def body(i, carry):
        base = pl.multiple_of(i * CONV_TILE, CONV_TILE)
        window = pad_ref[pl.ds(base, CONV_TILE + 2 * CONV_HALO), :]
        for r in range(1, SUBLANES):
            shift_ref[r - 1] = window[r:r + span, :]
        acc = jnp.zeros((CONV_TILE, ch), F32)
        for r in range(SUBLANES):
            for a in range((2 * CONV_HALO) // SUBLANES):
                t = SUBLANES * a + r - first
                if 0 <= t < CONV_WIDTH:
                    rows = slice(SUBLANES * a, SUBLANES * a + CONV_TILE)
                    tap = window[rows, :] if r == 0 else shift_ref[r - 1, rows, :]
                    acc = acc + tap * w_ref[t:t + 1, :]
        acc = acc + vec_ref[0:1, :]
        mu = jnp.mean(acc, axis=-1, keepdims=True)
        cen = acc - mu
        var = jnp.mean(cen * cen, axis=-1, keepdims=True)
        z = cen * lax.rsqrt(var + EPS) * vec_ref[1:2, :] + vec_ref[2:3, :]
        o_ref[0, pl.ds(base, CONV_TILE), :] = _silu(z).astype(o_ref.dtype)
        return carry

    lax.fori_loop(0, seq // CONV_TILE, body, 0)


def _conv_module(y, prm):
    b, l, ch = y.shape
    vec = jnp.stack([prm["conv_dw_b"], prm["conv_ln_g"], prm["conv_ln_b"]] + [jnp.zeros((ch,), F32)] * 5)
    w = jnp.pad(prm["conv_dw_w"], ((0, 1), (0, 0)))
    return pl.pallas_call(
        functools.partial(_conv_kernel, seq=l),
        grid=(b,),
        in_specs=[pl.BlockSpec((1, l, ch), lambda i: (i, 0, 0)), _resident(w.shape), _resident(vec.shape)],
        out_specs=pl.BlockSpec((1, l, ch), lambda i: (i, 0, 0)),
        out_shape=jax.ShapeDtypeStruct((b, l, ch), BF16),
        scratch_shapes=[pltpu.VMEM((l + 2 * CONV_HALO, ch), F32),
                        pltpu.VMEM((SUBLANES - 1, CONV_TILE + 2 * CONV_HALO - SUBLANES, ch), F32)],
        compiler_params=_params("parallel"),
    )(y, w, vec)


def _flat_ctx(a, batch):
    if a.shape[0] == batch:
        return a.reshape(1, batch * a.shape[1], a.shape[2])
    return a.reshape(batch, a.shape[1] // batch, a.shape[2])


def _trunk_layer(xl, xc, mods, prm, even, ctx_out):
    batch = xl.shape[0]
    mod_l, mod_c = mods
    xl = _half_ffn(xl, mod_l[:, 0:3], prm["ffn1_norm"], prm["ffn1_w_in"], prm["ffn1_w_out"])
    xc = _half_ffn(xc, mod_c[:, 0:3], prm["ffn1_norm"], prm["ffn1_w_in"], prm["ffn1_w_out"])
    w_out = prm["w_out"]
    half = w_out.shape[0] // 2
    parts_c = None
    if even:
        weights = _even_weights(prm)
        ql, kl, vl, yl = _even_proj(xl, mod_l[:, 3:5], weights, True)
        qc, kc, vc, yc = _even_proj(xc, mod_c[:, 3:5], weights, False)
        qc, kc, vc, yc = (_flat_ctx(a, batch) for a in (qc, kc, vc, yc))
        parts_l = (_pair_attention(ql, [(kc, vc), (kl, vl)], packed=False), _conv_module(yl, prm))
        w_a, w_b = w_out[:half], w_out[half:]
        if ctx_out:
            parts_c = (_pair_attention(qc, [(kc, vc)], packed=False), _conv_module(yc, prm))
    else:
        weights = _odd_weights(prm)
        cql, ckl, cvl, nql, nkl, nvl = _odd_proj(xl, mod_l[:, 3:5], weights, True)
        ctx_parts = [_flat_ctx(a, batch) for a in _odd_proj(xc, mod_c[:, 3:5], weights, False)]
        cqc, ckc, cvc, nqc, nkc, nvc = ctx_parts
        bias = _na_bias_table(prm["na_rpb"], xl.shape[1] // GRID_W)
        parts_l = (_pair_attention(cql, [(ckc, cvc), (ckl, cvl)], packed=True),
                   _neighbourhood_attention(nql, nkl, nvl, nkc, nvc, bias))
        order = np.concatenate([h * HEAD_DIM + np.arange(HEAD_DIM) for h in GQA_ORDER])
        w_a, w_b = w_out[:half][order], w_out[half:]
        if ctx_out:
            parts_c = (_pair_attention(cqc, [(ckc, cvc)], packed=True), _na_ctx_attention(nqc, nkc, nvc))

    def mixer_and_ffn2(x, mod, parts):
        mod4 = jnp.concatenate([mod[:, 6:9], mod[:, 5:6]], axis=1)
        return _half_ffn(x, mod4, prm["ffn2_norm"], prm["ffn2_w_in"], prm["ffn2_w_out"], mixer=parts + (w_a, w_b))

    xl = mixer_and_ffn2(xl, mod_l, parts_l)
    if ctx_out:
        xc = mixer_and_ffn2(xc, mod_c, tuple(_flat_ctx(p, batch) for p in parts_c))
    return xl, xc


def _layer_mods(c, c_ctx, prm):
    batch, d = c.shape
    rows = -(-(batch + 1) // 8) * 8
    cond = jnp.zeros((rows, d), F32).at[:batch].set(c).at[batch].set(c_ctx)
    mod = _modulation(cond, prm["mod_w"], prm["mod_b"]).reshape(rows, N_MOD, d)
    return mod[:batch], mod[batch:batch + 1]


def kernel(x, c, ctx, c_ctx,
           l0_mod_w, l0_mod_b, l0_ffn1_norm, l0_ffn1_w_in, l0_ffn1_w_out, l0_mix_norm, l0_w_in,
           l0_mla_q_norm, l0_mla_w_uq, l0_mla_kv_norm, l0_mla_w_ukv, l0_mla_q_gain, l0_mla_k_gain,
           l0_conv_glu_b, l0_conv_dw_w, l0_conv_dw_b, l0_conv_ln_g, l0_conv_ln_b,
           l0_w_out, l0_ffn2_norm, l0_ffn2_w_in, l0_ffn2_w_out,
           l1_mod_w, l1_mod_b, l1_ffn1_norm, l1_ffn1_w_in, l1_ffn1_w_out, l1_mix_norm, l1_w_in,
           l1_gqa_q_gain, l1_gqa_k_gain, l1_na_q_gain, l1_na_k_gain, l1_na_rpb,
           l1_w_out, l1_ffn2_norm, l1_ffn2_w_in, l1_ffn2_w_out):
    layers = (
        dict(mod_w=l0_mod_w, mod_b=l0_mod_b, ffn1_norm=l0_ffn1_norm, ffn1_w_in=l0_ffn1_w_in,
             ffn1_w_out=l0_ffn1_w_out, mix_norm=l0_mix_norm, w_in=l0_w_in,
             mla_q_norm=l0_mla_q_norm, mla_w_uq=l0_mla_w_uq, mla_kv_norm=l0_mla_kv_norm,
             mla_w_ukv=l0_mla_w_ukv, mla_q_gain=l0_mla_q_gain, mla_k_gain=l0_mla_k_gain,
             conv_glu_b=l0_conv_glu_b, conv_dw_w=l0_conv_dw_w, conv_dw_b=l0_conv_dw_b,
             conv_ln_g=l0_conv_ln_g, conv_ln_b=l0_conv_ln_b, w_out=l0_w_out,
             ffn2_norm=l0_ffn2_norm, ffn2_w_in=l0_ffn2_w_in, ffn2_w_out=l0_ffn2_w_out),
        dict(mod_w=l1_mod_w, mod_b=l1_mod_b, ffn1_norm=l1_ffn1_norm, ffn1_w_in=l1_ffn1_w_in,
             ffn1_w_out=l1_ffn1_w_out, mix_norm=l1_mix_norm, w_in=l1_w_in,
             gqa_q_gain=l1_gqa_q_gain, gqa_k_gain=l1_gqa_k_gain, na_q_gain=l1_na_q_gain,
             na_k_gain=l1_na_k_gain, na_rpb=l1_na_rpb, w_out=l1_w_out,
             ffn2_norm=l1_ffn2_norm, ffn2_w_in=l1_ffn2_w_in, ffn2_w_out=l1_ffn2_w_out),
    )
    batch = x.shape[0]
    xl, xc = x, _flat_ctx(ctx, batch)
    for i, prm in enumerate(layers):
        mods = _layer_mods(c, c_ctx, prm)
        xl, xc = _trunk_layer(xl, xc, mods, prm, even=(i % 2 == 0), ctx_out=(i < len(layers) - 1))
    return xl
```

```python
import functools
import math

import numpy as np
import jax
import jax.numpy as jnp
from jax import lax
from jax.experimental import pallas as pl
from jax.experimental.pallas import tpu as pltpu

F32 = jnp.float32
BF16 = jnp.bfloat16

D_MODEL = 1024
CTX_LEN = 256
GRID_W = 64
D_FF = 2816
N_MOD = 9
EPS = 1e-6
ROPE_BASE = 10000.0
MLA_HEADS = 8
MLA_NOPE = 64
MLA_ROPE = 32
MLA_V = 64
MLA_Q_LORA = 384
MLA_KV_LORA = 256
CONV_CH = 512
CONV_WIDTH = 31
HEAD_DIM = 64
GQA_HEADS = 8
GQA_KV_HEADS = 2
NA_HEADS = 8
WIN_H = 8
WIN_W = 16

LANES = 128
SUBLANES = 8
MXU_DIM = 256
VMEM_LIMIT = 56 * 1024 * 1024
LOG2E = math.log2(math.e)
NEG_BIG = -1e30

FFN_CHUNK = 256
TOKEN_TILE = 1024
PROJ_SUB = 512
FFN_TILE = 1024
ATT_TQ = 1024
ATT_SUB = 512
ATT_TK = 256
NA_ROWS = 4
NA_WIN_ROWS = 12
NA_GROUP = 4
NA_UNROLL = 2
CONV_TILE = 256
CONV_HALO = 16


def _params(*sem):
    return pltpu.CompilerParams(dimension_semantics=sem, vmem_limit_bytes=VMEM_LIMIT)


def _resident(shape):
    return pl.BlockSpec(shape, lambda *_: (0,) * len(shape), pipeline_mode=pl.Buffered(1))


def _silu(x):
    return x * jax.nn.sigmoid(x)


def _rms_rows(x):
    return x * lax.rsqrt(jnp.mean(x * x, axis=-1, keepdims=True) + EPS)


def _seg_sumsq(x, g):
    outs = []
    for p in range(x.shape[-1] // MXU_DIM):
        sq = x[:, p * MXU_DIM:(p + 1) * MXU_DIM]
        outs.append(jnp.dot((sq * sq).astype(BF16), g, preferred_element_type=F32))
    return outs[0] if len(outs) == 1 else jnp.concatenate(outs, axis=-1)


def _swap_halves(x, half, lo):
    lane = lax.broadcasted_iota(jnp.int32, x.shape, 1)
    first = ((lane - lo) & (2 * half - 1)) < half
    return jnp.where(first, pltpu.roll(x, LANES - half, 1), pltpu.roll(x, half, 1))


def _mod_kernel(c_ref, w_ref, b_ref, o_ref):
    a = _silu(c_ref[...]).astype(BF16)
    o_ref[...] = jnp.dot(a, w_ref[...].astype(BF16), preferred_element_type=F32) + b_ref[...]


def _modulation(cond, mod_w, mod_b):
    rows, d = cond.shape
    n = mod_w.shape[1]
    return pl.pallas_call(
        _mod_kernel,
        grid=(n // d,),
        in_specs=[pl.BlockSpec((rows, d), lambda j: (0, 0)),
                  pl.BlockSpec((d, d), lambda j: (0, j)),
                  pl.BlockSpec((1, d), lambda j: (0, j))],
        out_specs=pl.BlockSpec((rows, d), lambda j: (0, j)),
        out_shape=jax.ShapeDtypeStruct((rows, n), F32),
        compiler_params=_params("arbitrary"),
    )(cond, mod_w, mod_b.reshape(1, n))


def _ffn_kernel(*refs, mixer):
    if mixer:
        x_ref, mod_ref, g_ref, win_ref, wout_ref, pa_ref, pb_ref, wa_ref, wb_ref, o_ref, h_ref, a_ref = refs
    else:
        x_ref, mod_ref, g_ref, win_ref, wout_ref, o_ref, h_ref, a_ref = refs
    x = x_ref[0]
    mod = mod_ref[0]
    if mixer:
        mix = (jnp.dot(pa_ref[0], wa_ref[...], preferred_element_type=F32)
               + jnp.dot(pb_ref[0], wb_ref[...], preferred_element_type=F32))
        x = x + mod[3:4] * mix
    h = _rms_rows(x) * g_ref[...]
    h_ref[...] = (h * (1.0 + mod[1:2]) + mod[0:1]).astype(BF16)
    for c in range(D_FF // FFN_CHUNK):
        gate = jnp.dot(h_ref[...], win_ref[:, c * FFN_CHUNK:(c + 1) * FFN_CHUNK], preferred_element_type=F32)
        up = jnp.dot(h_ref[...], win_ref[:, D_FF + c * FFN_CHUNK:D_FF + (c + 1) * FFN_CHUNK], preferred_element_type=F32)
        a_ref[:, c * FFN_CHUNK:(c + 1) * FFN_CHUNK] = (_silu(gate) * up).astype(BF16)
    y = jnp.dot(a_ref[...], wout_ref[...], preferred_element_type=F32)
    o_ref[0] = x + (0.5 * mod[2:3]) * y


def _half_ffn(x, mod, norm_g, w_in, w_out, mixer=None):
    b, l, d = x.shape
    tm = min(FFN_TILE, l)
    tok = lambda w: pl.BlockSpec((1, tm, w), lambda i, j: (i, j, 0))
    in_specs = [tok(d), pl.BlockSpec((1,) + mod.shape[1:], lambda i, j: (i, 0, 0)),
                _resident((1, d)), _resident(w_in.shape), _resident(w_out.shape)]
    args = [x, mod, norm_g.reshape(1, d), w_in.astype(BF16), w_out.astype(BF16)]
    if mixer is not None:
        part_a, part_b, w_a, w_b = mixer
        in_specs += [tok(part_a.shape[-1]), tok(part_b.shape[-1]), _resident(w_a.shape), _resident(w_b.shape)]
        args += [part_a, part_b, w_a.astype(BF16), w_b.astype(BF16)]
    return pl.pallas_call(
        functools.partial(_ffn_kernel, mixer=mixer is not None),
        grid=(b, l // tm),
        in_specs=in_specs,
        out_specs=tok(d),
        out_shape=jax.ShapeDtypeStruct((b, l, d), F32),
        scratch_shapes=[pltpu.VMEM((tm, d), BF16), pltpu.VMEM((tm, D_FF), BF16)],
        compiler_params=_params("parallel", "parallel"),
    )(*args)


E_KR = MLA_KV_LORA
E_CQ = E_KR + LANES
E_CA = E_CQ + MLA_Q_LORA
E_CG = E_CA + CONV_CH
E_COLS = E_CG + CONV_CH
HEADS_W = MLA_HEADS * LANES


def _even_proj_kernel(x_ref, mod_ref, vec_ref, win_ref, wuk_ref, wuv_ref, wuq_ref, g_ref, c_ref, s_ref,
                      q_ref, k_ref, v_ref, y_ref):
    rows = min(PROJ_SUB, x_ref.shape[1])
    for r0 in range(0, x_ref.shape[1], rows):
        tok = lambda ref: ref.at[:, r0:r0 + rows, :]
        _even_proj_tile(tok(x_ref), mod_ref, vec_ref, win_ref, wuk_ref, wuv_ref, wuq_ref, g_ref,
                        c_ref.at[r0:r0 + rows, :], s_ref.at[r0:r0 + rows, :],
                        tok(q_ref), tok(k_ref), tok(v_ref), tok(y_ref))


def _even_proj_tile(x_ref, mod_ref, vec_ref, win_ref, wuk_ref, wuv_ref, wuq_ref, g_ref, c_ref, s_ref,
                    q_ref, k_ref, v_ref, y_ref):
    x = x_ref[0]
    mod = mod_ref[0]
    h = _rms_rows(x) * vec_ref[0:1, :]
    h = (h * (1.0 + mod[1:2]) + mod[0:1]).astype(BF16)
    pp = jnp.dot(h, win_ref[...], preferred_element_type=F32)
    cos = c_ref[...]
    sin = s_ref[...]
    g = g_ref[...]

    cq = (_rms_rows(pp[:, E_CQ:E_CA]) * vec_ref[2:3, 0:MLA_Q_LORA]).astype(BF16)
    q = jnp.dot(cq, wuq_ref[...], preferred_element_type=F32)
    q = q * lax.rsqrt(_seg_sumsq(q, g) * vec_ref[6:7, :] + EPS) * vec_ref[5:6, :]
    for hd in range(MLA_HEADS):
        sl = slice(hd * LANES, (hd + 1) * LANES)
        qh = q[:, sl]
        qh = qh * cos + _swap_halves(qh, MLA_ROPE // 2, MLA_NOPE) * sin
        q_ref[0, :, sl] = qh.astype(BF16)

    ckv = (_rms_rows(pp[:, 0:E_KR]) * vec_ref[1:2, 0:MLA_KV_LORA]).astype(BF16)
    kn = jnp.dot(ckv, wuk_ref[...], preferred_element_type=F32)
    vv = jnp.dot(ckv, wuv_ref[...], preferred_element_type=F32)
    kn = kn * lax.rsqrt(_seg_sumsq(kn, g) * vec_ref[6:7, :] + EPS) * vec_ref[3:4, :]
    kr = pp[:, E_KR:E_CQ]
    kr = kr * lax.rsqrt(jnp.sum(kr * kr, axis=-1, keepdims=True) * (1.0 / MLA_ROPE) + EPS) * vec_ref[4:5, 0:LANES]
    kr = kr * cos + _swap_halves(kr, MLA_ROPE // 2, MLA_NOPE) * sin
    v_ref[0] = (vv + vec_ref[8:9, :]).astype(BF16)
    for hd in range(MLA_HEADS):
        sl = slice(hd * LANES, (hd + 1) * LANES)
        k_ref[0, :, sl] = (kn[:, sl] + kr).astype(BF16)

    glu_b = vec_ref[7:8, :]
    y_ref[0] = (pp[:, E_CA:E_CG] + glu_b[:, 0:CONV_CH]) * jax.nn.sigmoid(pp[:, E_CG:E_COLS] + glu_b[:, CONV_CH:])


def _even_weights(prm):
    d = D_MODEL
    ev = np.arange(0, MLA_ROPE, 2)
    od = np.arange(1, MLA_ROPE, 2)
    zero_col = prm["w_in"].shape[1]
    cols = np.concatenate([
        np.arange(MLA_KV_LORA),
        np.full(MLA_NOPE, zero_col), MLA_KV_LORA + ev, MLA_KV_LORA + od, np.full(LANES - MLA_NOPE - MLA_ROPE, zero_col),
        np.arange(MLA_KV_LORA + MLA_ROPE, zero_col)])
    w_ext = jnp.concatenate([prm["w_in"], jnp.zeros((d, 1), F32)], axis=1)
    win = jnp.take(w_ext, cols, axis=1).astype(BF16)

    qd = MLA_NOPE + MLA_ROPE
    zq = MLA_HEADS * qd
    qcols = np.concatenate([np.concatenate([h * qd + np.arange(MLA_NOPE), h * qd + MLA_NOPE + ev, h * qd + MLA_NOPE + od,
                                            np.full(LANES - qd, zq)]) for h in range(MLA_HEADS)])
    wuq = jnp.take(jnp.concatenate([prm["mla_w_uq"], jnp.zeros((MLA_Q_LORA, 1), F32)], axis=1), qcols, axis=1).astype(BF16)
    kvd = MLA_NOPE + MLA_V
    zk = MLA_HEADS * kvd
    kcols = np.concatenate([np.concatenate([h * kvd + np.arange(MLA_NOPE), np.full(LANES - MLA_NOPE, zk)])
                            for h in range(MLA_HEADS)])
    vcols = np.concatenate([np.concatenate([h * kvd + MLA_NOPE + np.arange(MLA_V), np.full(LANES - MLA_V, zk)])
                            for h in range(MLA_HEADS)])
    wukv = jnp.concatenate([prm["mla_w_ukv"], jnp.zeros((MLA_KV_LORA, 1), F32)], axis=1)
    wuk = jnp.take(wukv, kcols, axis=1).astype(BF16)
    wuv = jnp.take(wukv, vcols, axis=1).astype(BF16)

    def row(v):
        return jnp.pad(v.astype(F32), (0, HEADS_W - v.shape[0]))

    rope_perm = np.concatenate([MLA_NOPE + ev, MLA_NOPE + od])
    pad_q = jnp.zeros((LANES - qd,), F32)
    kg, qg = prm["mla_k_gain"], prm["mla_q_gain"]
    q_scale = (qd ** -0.5) * LOG2E
    inv_cnt = np.tile(np.concatenate([np.full(MLA_NOPE, 1.0 / MLA_NOPE), np.full(MLA_ROPE, 1.0 / MLA_ROPE),
                                      np.ones(LANES - qd)]), MLA_HEADS).astype(np.float32)
    ones_lane = np.tile((np.arange(LANES) == MLA_V).astype(np.float32), MLA_HEADS)
    vec = jnp.stack([
        row(prm["mix_norm"]),
        row(prm["mla_kv_norm"]),
        row(prm["mla_q_norm"]),
        jnp.tile(jnp.concatenate([kg[:MLA_NOPE], jnp.zeros((LANES - MLA_NOPE,), F32)]), MLA_HEADS),
        row(jnp.concatenate([jnp.zeros((MLA_NOPE,), F32), kg[rope_perm], pad_q])),
        jnp.tile(jnp.concatenate([qg[:MLA_NOPE], qg[rope_perm], pad_q]), MLA_HEADS) * q_scale,
        jnp.asarray(inv_cnt),
        row(prm["conv_glu_b"]),
        jnp.asarray(ones_lane),
    ] + [jnp.zeros((HEADS_W,), F32)] * 7)

    seg = np.arange(MXU_DIM)
    seg_id = (seg // LANES) * 4 + np.where(seg % LANES < MLA_NOPE, 0, np.where(seg % LANES < qd, 1, 2))
    gmat = jnp.asarray((seg_id[:, None] == seg_id[None, :]).astype(np.float32)).astype(BF16)
    return win, wuk, wuv, wuq, vec, gmat


def _mla_rope_tables(n_tokens, use_rope):
    ones = jnp.ones((n_tokens, MLA_NOPE), F32)
    tail = jnp.ones((n_tokens, LANES - MLA_NOPE - MLA_ROPE), F32)
    if not use_rope:
        c = jnp.ones((n_tokens, LANES), F32)
        return c, jnp.zeros_like(c)
    cos, sin = _axial_angles(n_tokens, MLA_ROPE)
    c = jnp.concatenate([ones, cos, cos, tail], axis=-1)
    s = jnp.concatenate([0 * ones, -sin, sin, 0 * tail], axis=-1)
    return c, s


def _axial_angles(n_tokens, dim):
    t = jnp.arange(n_tokens)
    row = (t // GRID_W).astype(F32)
    col = (t % GRID_W).astype(F32)
    n_pairs = dim // 4
    inv = ROPE_BASE ** (-jnp.arange(n_pairs, dtype=F32) / n_pairs)
    ang = jnp.concatenate([row[:, None] * inv, col[:, None] * inv], axis=-1)
    return jnp.cos(ang), jnp.sin(ang)


def _even_proj(x, mod2, weights, use_rope):
    win, wuk, wuv, wuq, vec, gmat = weights
    b, l, d = x.shape
    tm = min(TOKEN_TILE, l)
    cos, sin = _mla_rope_tables(l, use_rope)
    tok = lambda w: pl.BlockSpec((1, tm, w), lambda i, j: (i, j, 0))
    return pl.pallas_call(
        _even_proj_kernel,
        grid=(b, l // tm),
        in_specs=[tok(d),
                  pl.BlockSpec((1, 2, d), lambda i, j: (i, 0, 0)),
                  _resident(vec.shape), _resident(win.shape), _resident(wuk.shape), _resident(wuv.shape),
                  _resident(wuq.shape), _resident(gmat.shape),
                  pl.BlockSpec((tm, LANES), lambda i, j: (j, 0)),
                  pl.BlockSpec((tm, LANES), lambda i, j: (j, 0))],
        out_specs=[tok(HEADS_W), tok(HEADS_W), tok(HEADS_W), tok(CONV_CH)],
        out_shape=[jax.ShapeDtypeStruct((b, l, HEADS_W), BF16)] * 3 + [jax.ShapeDtypeStruct((b, l, CONV_CH), F32)],
        compiler_params=_params("parallel", "parallel"),
    )(x, mod2, vec, win, wuk, wuv, wuq, gmat, cos, sin)


O_CK = 0
O_CV = O_CK + GQA_KV_HEADS * HEAD_DIM
O_NK = O_CV + GQA_KV_HEADS * LANES
O_NV = O_NK + NA_HEADS * HEAD_DIM
O_CQ = O_NV + NA_HEADS * LANES
O_NQ = O_CQ + GQA_HEADS * HEAD_DIM
O_COLS = O_NQ + NA_HEADS * HEAD_DIM
GQA_ORDER = (0, 4, 1, 5, 2, 6, 3, 7)


def _rope64(x, cos, sin):
    outs = []
    for p in range(x.shape[-1] // LANES):
        xs = x[:, p * LANES:(p + 1) * LANES]
        outs.append(xs * cos + _swap_halves(xs, HEAD_DIM // 2, 0) * sin)
    return outs[0] if len(outs) == 1 else jnp.concatenate(outs, axis=-1)


def _odd_proj_kernel(x_ref, mod_ref, vec_ref, win_ref, g_ref, c_ref, s_ref, *out_refs):
    rows = min(PROJ_SUB, x_ref.shape[1])
    for r0 in range(0, x_ref.shape[1], rows):
        tok = lambda ref: ref.at[:, r0:r0 + rows, :]
        _odd_proj_tile(tok(x_ref), mod_ref, vec_ref, win_ref, g_ref,
                       c_ref.at[r0:r0 + rows, :], s_ref.at[r0:r0 + rows, :], *[tok(o) for o in out_refs])


def _odd_proj_tile(x_ref, mod_ref, vec_ref, win_ref, g_ref, c_ref, s_ref,
                   cq_ref, ck_ref, cv_ref, nq_ref, nk_ref, nv_ref):
    x = x_ref[0]
    mod = mod_ref[0]
    h = _rms_rows(x) * vec_ref[0:1, :]
    h = (h * (1.0 + mod[1:2]) + mod[0:1]).astype(BF16)
    pp = jnp.dot(h, win_ref[...], preferred_element_type=F32)
    cos = c_ref[...]
    sin = s_ref[...]
    g = g_ref[...]
    inv = 1.0 / HEAD_DIM

    def head_norm(v, gain):
        if v.shape[-1] == LANES:
            ss = jnp.dot((v * v).astype(BF16), g[0:LANES, 0:LANES], preferred_element_type=F32)
        else:
            ss = _seg_sumsq(v, g)
        return v * lax.rsqrt(ss * inv + EPS) * gain

    wq = GQA_HEADS * HEAD_DIM
    cq = head_norm(pp[:, O_CQ:O_NQ], vec_ref[3:4, 0:wq])
    cq_ref[0] = _rope64(cq, cos, sin).astype(BF16)
    ck = head_norm(pp[:, O_CK:O_CV], vec_ref[1:2, 0:LANES])
    ck_ref[0] = _rope64(ck, cos, sin).astype(BF16)
    nq_ref[0] = head_norm(pp[:, O_NQ:O_COLS], vec_ref[4:5, 0:wq]).astype(BF16)
    nk_ref[0] = head_norm(pp[:, O_NK:O_NV], vec_ref[2:3, 0:wq]).astype(BF16)
    cv_ref[0] = (pp[:, O_CV:O_NK] + vec_ref[5:6, 0:GQA_KV_HEADS * LANES]).astype(BF16)
    nv_ref[0] = (pp[:, O_NV:O_CQ] + vec_ref[5:6, :]).astype(BF16)


def _odd_weights(prm):
    d = D_MODEL
    ev = np.arange(0, HEAD_DIM, 2)
    od = np.arange(1, HEAD_DIM, 2)
    kvw = GQA_KV_HEADS * HEAD_DIM
    naw = NA_HEADS * HEAD_DIM
    zero_col = prm["w_in"].shape[1]
    pad64 = np.full(LANES - HEAD_DIM, zero_col)
    q0 = 2 * kvw + 2 * naw
    cols = np.concatenate(
        [np.concatenate([g * HEAD_DIM + ev, g * HEAD_DIM + od]) for g in range(GQA_KV_HEADS)]
        + [np.concatenate([kvw + g * HEAD_DIM + np.arange(HEAD_DIM), pad64]) for g in range(GQA_KV_HEADS)]
        + [2 * kvw + np.arange(naw)]
        + [np.concatenate([2 * kvw + naw + h * HEAD_DIM + np.arange(HEAD_DIM), pad64]) for h in range(NA_HEADS)]
        + [np.concatenate([q0 + h * HEAD_DIM + ev, q0 + h * HEAD_DIM + od]) for h in GQA_ORDER]
        + [q0 + GQA_HEADS * HEAD_DIM + np.arange(naw)])
    assert cols.shape[0] == O_COLS
    w_ext = jnp.concatenate([prm["w_in"], jnp.zeros((d, 1), F32)], axis=1)
    win = jnp.take(w_ext, cols, axis=1).astype(BF16)

    def row(v):
        return jnp.pad(v.astype(F32), (0, HEADS_W - v.shape[0]))

    perm = np.concatenate([ev, od])
    q_scale = (HEAD_DIM ** -0.5) * LOG2E
    ones_lane = np.tile((np.arange(LANES) == HEAD_DIM).astype(np.float32), NA_HEADS)
    vec = jnp.stack([
        row(prm["mix_norm"]),
        row(jnp.tile(prm["gqa_k_gain"][perm], GQA_KV_HEADS)),
        row(jnp.tile(prm["na_k_gain"], NA_HEADS)),
        row(jnp.tile(prm["gqa_q_gain"][perm], GQA_HEADS) * q_scale),
        row(jnp.tile(prm["na_q_gain"], NA_HEADS) * q_scale),
        jnp.asarray(ones_lane),
    ] + [jnp.zeros((HEADS_W,), F32)] * 2)
    seg_id = np.arange(MXU_DIM) // HEAD_DIM
    gmat = jnp.asarray((seg_id[:, None] == seg_id[None, :]).astype(np.float32)).astype(BF16)
    return win, vec, gmat


def _gqa_rope_tables(n_tokens, use_rope):
    if not use_rope:
        c = jnp.ones((n_tokens, LANES), F32)
        return c, jnp.zeros_like(c)
    cos, sin = _axial_angles(n_tokens, HEAD_DIM)
    return jnp.concatenate([cos, cos, cos, cos], axis=-1), jnp.concatenate([-sin, sin, -sin, sin], axis=-1)


def _odd_proj(x, mod2, weights, use_rope):
    win, vec, gmat = weights
    b, l, d = x.shape
    tm = min(TOKEN_TILE, l)
    cos, sin = _gqa_rope_tables(l, use_rope)
    tok = lambda w: pl.BlockSpec((1, tm, w), lambda i, j: (i, j, 0))
    widths = (GQA_HEADS * HEAD_DIM, GQA_KV_HEADS * HEAD_DIM, GQA_KV_HEADS * LANES,
              NA_HEADS * HEAD_DIM, NA_HEADS * HEAD_DIM, NA_HEADS * LANES)
    return pl.pallas_call(
        _odd_proj_kernel,
        grid=(b, l // tm),
        in_specs=[tok(d),
                  pl.BlockSpec((1, 2, d), lambda i, j: (i, 0, 0)),
                  _resident(vec.shape), _resident(win.shape), _resident(gmat.shape),
                  pl.BlockSpec((tm, LANES), lambda i, j: (j, 0)),
                  pl.BlockSpec((tm, LANES), lambda i, j: (j, 0))],
        out_specs=[tok(w) for w in widths],
        out_shape=[jax.ShapeDtypeStruct((b, l, w), BF16) for w in widths],
        compiler_params=_params("parallel", "parallel"),
    )(x, mod2, vec, win, gmat, cos, sin)


def _pair_queries(q, packed):
    if not packed:
        return q[:, 0:LANES], q[:, LANES:2 * LANES]
    lane = lax.broadcasted_iota(jnp.int32, q.shape, 1)
    zero = jnp.zeros_like(q)
    return jnp.where(lane < HEAD_DIM, q, zero), jnp.where(lane >= HEAD_DIM, q, zero)


def _pair_output(acc_a, acc_b):
    oa = acc_a / acc_a[:, HEAD_DIM:HEAD_DIM + 1]
    ob = acc_b / acc_b[:, HEAD_DIM:HEAD_DIM + 1]
    lane = lax.broadcasted_iota(jnp.int32, oa.shape, 1)
    return jnp.where(lane < HEAD_DIM, oa, pltpu.roll(ob, HEAD_DIM, 1))


def _score(q, k):
    return lax.dot_general(q, k, (((1,), (1,)), ((), ())), preferred_element_type=F32)


def _attn_kernel(*refs, n_src, packed, src_len):
    q_ref = refs[0]
    kv_refs = refs[1:1 + 2 * n_src]
    o_ref = refs[1 + 2 * n_src]
    rows = min(ATT_SUB, q_ref.shape[1])
    for r0 in range(0, q_ref.shape[1], rows):
        qs = _pair_queries(q_ref[0, r0:r0 + rows, :], packed)
        m = [jnp.full((rows, 1), NEG_BIG, F32) for _ in range(2)]
        acc = [jnp.zeros((rows, LANES), F32) for _ in range(2)]
        for s in range(n_src):
            k_ref, v_ref = kv_refs[2 * s], kv_refs[2 * s + 1]
            tk = min(ATT_TK, src_len[s])
            for blk in range(src_len[s] // tk):
                rs = slice(blk * tk, (blk + 1) * tk)
                for hd in range(2):
                    ks = slice(0, LANES) if packed else slice(hd * LANES, (hd + 1) * LANES)
                    sc = _score(qs[hd], k_ref[0, rs, ks])
                    m_new = jnp.maximum(m[hd], jnp.max(sc, axis=-1, keepdims=True))
                    p = jnp.exp2(sc - m_new).astype(BF16)
                    pv = jnp.dot(p, v_ref[0, rs, hd * LANES:(hd + 1) * LANES], preferred_element_type=F32)
                    acc[hd] = acc[hd] * jnp.exp2(m[hd] - m_new) + pv
                    m[hd] = m_new
        o_ref[0, r0:r0 + rows, :] = _pair_output(acc[0], acc[1]).astype(o_ref.dtype)


def _pair_attention(q, sources, packed):
    b, lq, qw = q.shape
    wq = LANES if packed else 2 * LANES
    n_pairs = qw // wq
    tq = min(ATT_TQ, lq)
    in_specs = [pl.BlockSpec((1, tq, wq), lambda i, p, j: (i, j, p))]
    args = [q]
    for k, v in sources:
        lk = k.shape[1]
        if packed:
            in_specs.append(pl.BlockSpec((1, lk, LANES), lambda i, p, j: (i, 0, 0)))
            in_specs.append(pl.BlockSpec((1, lk, 2 * LANES), lambda i, p, j: (i, 0, 0)))
        else:
            in_specs.append(pl.BlockSpec((1, lk, 2 * LANES), lambda i, p, j: (i, 0, p)))
            in_specs.append(pl.BlockSpec((1, lk, 2 * LANES), lambda i, p, j: (i, 0, p)))
        args += [k, v]
    kern = functools.partial(_attn_kernel, n_src=len(sources), packed=packed,
                             src_len=tuple(k.shape[1] for k, _ in sources))
    return pl.pallas_call(
        kern,
        grid=(b, n_pairs, lq // tq),
        in_specs=in_specs,
        out_specs=pl.BlockSpec((1, tq, LANES), lambda i, p, j: (i, j, p)),
        out_shape=jax.ShapeDtypeStruct((b, lq, n_pairs * LANES), BF16),
        compiler_params=_params("parallel", "parallel", "arbitrary"),
    )(*args)


def _na_ctx_kernel(q_ref, k_ref, v_ref, o_ref):
    qs = _pair_queries(q_ref[0], True)
    accs = []
    for hd in range(2):
        sc = _score(qs[hd], k_ref[0])
        p = jnp.exp2(sc - jnp.max(sc, axis=-1, keepdims=True)).astype(BF16)
        accs.append(jnp.dot(p, v_ref[0, :, hd * LANES:(hd + 1) * LANES], preferred_element_type=F32))
    o_ref[0] = _pair_output(accs[0], accs[1]).astype(o_ref.dtype)


def _na_ctx_attention(q, k, v):
    b, l, w = q.shape
    n_pairs = w // LANES
    return pl.pallas_call(
        _na_ctx_kernel,
        grid=(b, n_pairs),
        in_specs=[pl.BlockSpec((1, l, LANES), lambda i, p: (i, 0, p)),
                  pl.BlockSpec((1, l, LANES), lambda i, p: (i, 0, p)),
                  pl.BlockSpec((1, l, 2 * LANES), lambda i, p: (i, 0, p))],
        out_specs=pl.BlockSpec((1, l, LANES), lambda i, p: (i, 0, p)),
        out_shape=jax.ShapeDtypeStruct((b, l, w), BF16),
        compiler_params=_params("parallel", "parallel"),
    )(q, k, v)


def _na_kernel(q_ref, k_ref, v_ref, kc_ref, vc_ref, bias_ref, o_ref, *, n_blocks, grid_rows):
    blk_q = NA_ROWS * GRID_W
    win = NA_WIN_ROWS * GRID_W
    kc = kc_ref[0]

    def body(g, carry):
        work = []
        for u in range(NA_GROUP):
            i = g * NA_GROUP + u
            q0 = pl.multiple_of(i * blk_q, blk_q)
            ws = jnp.clip(i * NA_ROWS - WIN_H // 2, 0, grid_rows - NA_WIN_ROWS)
            k0 = pl.multiple_of(ws * GRID_W, GRID_W)
            cfg = jnp.where(i == 0, 0, jnp.where(i == n_blocks - 1, 2, 1))
            qs = _pair_queries(q_ref[0, pl.ds(q0, blk_q), :], True)
            kw = k_ref[0, pl.ds(k0, win), :]
            sc = [(_score(qs[hd], kw) + bias_ref[0, cfg, hd], _score(qs[hd], kc)) for hd in range(2)]
            work.append((q0, k0, sc))
        for q0, k0, sc in work:
            accs = []
            for hd in range(2):
                s_nb, s_c = sc[hd]
                m = jnp.maximum(jnp.max(s_nb, axis=-1, keepdims=True), jnp.max(s_c, axis=-1, keepdims=True))
                p_nb = jnp.exp2(s_nb - m).astype(BF16)
                p_c = jnp.exp2(s_c - m).astype(BF16)
                vs = slice(hd * LANES, (hd + 1) * LANES)
                accs.append(jnp.dot(p_nb, v_ref[0, pl.ds(k0, win), vs], preferred_element_type=F32)
                            + jnp.dot(p_c, vc_ref[0, :, vs], preferred_element_type=F32))
            o_ref[0, pl.ds(q0, blk_q), :] = _pair_output(accs[0], accs[1]).astype(o_ref.dtype)
        return carry

    lax.fori_loop(0, n_blocks // NA_GROUP, body, 0, unroll=NA_UNROLL)


def _na_bias_table(rpb, grid_rows):
    n_blocks = grid_rows // NA_ROWS
    assert grid_rows % NA_ROWS == 0 and grid_rows >= NA_WIN_ROWS + NA_ROWS
    n_dr, n_dc = 2 * WIN_H - 1, 2 * WIN_W - 1
    ri = np.arange(NA_ROWS)[:, None]
    kj = np.arange(NA_WIN_ROWS)[None, :]
    row_sel, row_valid = [], []
    for blk in (0, 1, n_blocks - 1):
        ws = int(np.clip(blk * NA_ROWS - WIN_H // 2, 0, grid_rows - NA_WIN_ROWS))
        r = blk * NA_ROWS + ri
        krow = ws + kj
        rs = np.clip(r - WIN_H // 2, 0, grid_rows - WIN_H)
        row_valid.append((krow >= rs) & (krow < rs + WIN_H))
        row_sel.append(np.eye(n_dr, dtype=np.float32)[np.clip(krow - r + (WIN_H - 1), 0, n_dr - 1)])
    row_sel = np.stack(row_sel)
    row_valid = np.stack(row_valid)
    cq = np.arange(GRID_W)[:, None]
    ck = np.arange(GRID_W)[None, :]
    cs = np.clip(cq - WIN_W // 2, 0, GRID_W - WIN_W)
    col_valid = (ck >= cs) & (ck < cs + WIN_W)
    col_sel = np.eye(n_dc, dtype=np.float32)[np.clip(ck - cq + (WIN_W - 1), 0, n_dc - 1)]
    rpb2 = rpb.reshape(rpb.shape[0] // 2, 2, n_dr, n_dc) * LOG2E
    t = jnp.einsum("cijd,pade->pcaije", row_sel, rpb2, precision=lax.Precision.HIGHEST)
    t = jnp.einsum("pcaije,qke->pcaiqjk", t, col_sel, precision=lax.Precision.HIGHEST)
    valid = row_valid[:, None, :, None, :, None] & col_valid[None, None, None, :, None, :]
    tab = jnp.where(jnp.asarray(valid)[None], t, NEG_BIG)
    return tab.reshape(tab.shape[:3] + (NA_ROWS * GRID_W, NA_WIN_ROWS * GRID_W))


def _neighbourhood_attention(q, k, v, kc, vc, bias):
    b, s, w = q.shape
    n_pairs = w // LANES
    grid_rows = s // GRID_W
    n_blocks = grid_rows // NA_ROWS
    lc = kc.shape[1]
    return pl.pallas_call(
        functools.partial(_na_kernel, n_blocks=n_blocks, grid_rows=grid_rows),
        grid=(b, n_pairs),
        in_specs=[pl.BlockSpec((1, s, LANES), lambda i, p: (i, 0, p)),
                  pl.BlockSpec((1, s, LANES), lambda i, p: (i, 0, p)),
                  pl.BlockSpec((1, s, 2 * LANES), lambda i, p: (i, 0, p)),
                  pl.BlockSpec((1, lc, LANES), lambda i, p: (i, 0, p)),
                  pl.BlockSpec((1, lc, 2 * LANES), lambda i, p: (i, 0, p)),
                  pl.BlockSpec((1,) + bias.shape[1:], lambda i, p: (p, 0, 0, 0, 0))],
        out_specs=pl.BlockSpec((1, s, LANES), lambda i, p: (i, 0, p)),
        out_shape=jax.ShapeDtypeStruct((b, s, w), BF16),
        compiler_params=_params("parallel", "arbitrary"),
    )(q, k, v, kc, vc, bias)


def _conv_kernel(y_ref, w_ref, vec_ref, o_ref, pad_ref, shift_ref, *, seq):
    ch = y_ref.shape[-1]
    zeros = jnp.zeros((CONV_HALO, ch), F32)
    pad_ref[0:CONV_HALO, :] = zeros
    pad_ref[CONV_HALO + seq:2 * CONV_HALO + seq, :] = zeros
    pad_ref[CONV_HALO:CONV_HALO + seq, :] = y_ref[0]
    first = CONV_HALO - CONV_WIDTH // 2

    span = CONV_TILE + 2 * CONV_HALO - SUBLANES

    def body(i, carry):
        base = pl.multiple_of(i * CONV_TILE, CONV_TILE)
        window = pad_ref[pl.ds(base, CONV_TILE + 2 * CONV_HALO), :]
        for r in range(1, SUBLANES):
            shift_ref[r - 1] = window[r:r + span, :]
        acc = jnp.zeros((CONV_TILE, ch), F32)
        for r in range(SUBLANES):
            for a in range((2 * CONV_HALO) // SUBLANES):
                t = SUBLANES * a + r - first
                if 0 <= t < CONV_WIDTH:
                    rows = slice(SUBLANES * a, SUBLANES * a + CONV_TILE)
                    tap = window[rows, :] if r == 0 else shift_ref[r - 1, rows, :]
                    acc = acc + tap * w_ref[t:t + 1, :]
        acc = acc + vec_ref[0:1, :]
        mu = jnp.mean(acc, axis=-1, keepdims=True)
        cen = acc - mu
        var = jnp.mean(cen * cen, axis=-1, keepdims=True)
        z = cen * lax.rsqrt(var + EPS) * vec_ref[1:2, :] + vec_ref[2:3, :]
        o_ref[0, pl.ds(base, CONV_TILE), :] = _silu(z).astype(o_ref.dtype)
        return carry

    lax.fori_loop(0, seq // CONV_TILE, body, 0)


def _conv_module(y, prm):
    b, l, ch = y.shape
    vec = jnp.stack([prm["conv_dw_b"], prm["conv_ln_g"], prm["conv_ln_b"]] + [jnp.zeros((ch,), F32)] * 5)
    w = jnp.pad(prm["conv_dw_w"], ((0, 1), (0, 0)))
    return pl.pallas_call(
        functools.partial(_conv_kernel, seq=l),
        grid=(b,),
        in_specs=[pl.BlockSpec((1, l, ch), lambda i: (i, 0, 0)), _resident(w.shape), _resident(vec.shape)],
        out_specs=pl.BlockSpec((1, l, ch), lambda i: (i, 0, 0)),
        out_shape=jax.ShapeDtypeStruct((b, l, ch), BF16),
        scratch_shapes=[pltpu.VMEM((l + 2 * CONV_HALO, ch), F32),
                        pltpu.VMEM((SUBLANES - 1, CONV_TILE + 2 * CONV_HALO - SUBLANES, ch), F32)],
        compiler_params=_params("parallel"),
    )(y, w, vec)


def _flat_ctx(a, batch):
    if a.shape[0] == batch:
        return a.reshape(1, batch * a.shape[1], a.shape[2])
    return a.reshape(batch, a.shape[1] // batch, a.shape[2])


def _trunk_layer(xl, xc, mods, prm, even, ctx_out):
    batch = xl.shape[0]
    mod_l, mod_c = mods
    xl = _half_ffn(xl, mod_l[:, 0:3], prm["ffn1_norm"], prm["ffn1_w_in"], prm["ffn1_w_out"])
    xc = _half_ffn(xc, mod_c[:, 0:3], prm["ffn1_norm"], prm["ffn1_w_in"], prm["ffn1_w_out"])
    w_out = prm["w_out"]
    half = w_out.shape[0] // 2
    parts_c = None
    if even:
        weights = _even_weights(prm)
        ql, kl, vl, yl = _even_proj(xl, mod_l[:, 3:5], weights, True)
        qc, kc, vc, yc = _even_proj(xc, mod_c[:, 3:5], weights, False)
        qc, kc, vc, yc = (_flat_ctx(a, batch) for a in (qc, kc, vc, yc))
        parts_l = (_pair_attention(ql, [(kc, vc), (kl, vl)], packed=False), _conv_module(yl, prm))
        w_a, w_b = w_out[:half], w_out[half:]
        if ctx_out:
            parts_c = (_pair_attention(qc, [(kc, vc)], packed=False), _conv_module(yc, prm))
    else:
        weights = _odd_weights(prm)
        cql, ckl, cvl, nql, nkl, nvl = _odd_proj(xl, mod_l[:, 3:5], weights, True)
        ctx_parts = [_flat_ctx(a, batch) for a in _odd_proj(xc, mod_c[:, 3:5], weights, False)]
        cqc, ckc, cvc, nqc, nkc, nvc = ctx_parts
        bias = _na_bias_table(prm["na_rpb"], xl.shape[1] // GRID_W)
        parts_l = (_pair_attention(cql, [(ckc, cvc), (ckl, cvl)], packed=True),
                   _neighbourhood_attention(nql, nkl, nvl, nkc, nvc, bias))
        order = np.concatenate([h * HEAD_DIM + np.arange(HEAD_DIM) for h in GQA_ORDER])
        w_a, w_b = w_out[:half][order], w_out[half:]
        if ctx_out:
            parts_c = (_pair_attention(cqc, [(ckc, cvc)], packed=True), _na_ctx_attention(nqc, nkc, nvc))

    def mixer_and_ffn2(x, mod, parts):
        mod4 = jnp.concatenate([mod[:, 6:9], mod[:, 5:6]], axis=1)
        return _half_ffn(x, mod4, prm["ffn2_norm"], prm["ffn2_w_in"], prm["ffn2_w_out"], mixer=parts + (w_a, w_b))

    xl = mixer_and_ffn2(xl, mod_l, parts_l)
    if ctx_out:
        xc = mixer_and_ffn2(xc, mod_c, tuple(_flat_ctx(p, batch) for p in parts_c))
    return xl, xc


def _layer_mods(c, c_ctx, prm):
    batch, d = c.shape
    rows = -(-(batch + 1) // 8) * 8
    cond = jnp.zeros((rows, d), F32).at[:batch].set(c).at[batch].set(c_ctx)
    mod = _modulation(cond, prm["mod_w"], prm["mod_b"]).reshape(rows, N_MOD, d)
    return mod[:batch], mod[batch:batch + 1]


def kernel(x, c, ctx, c_ctx,
           l0_mod_w, l0_mod_b, l0_ffn1_norm, l0_ffn1_w_in, l0_ffn1_w_out, l0_mix_norm, l0_w_in,
           l0_mla_q_norm, l0_mla_w_uq, l0_mla_kv_norm, l0_mla_w_ukv, l0_mla_q_gain, l0_mla_k_gain,
           l0_conv_glu_b, l0_conv_dw_w, l0_conv_dw_b, l0_conv_ln_g, l0_conv_ln_b,
           l0_w_out, l0_ffn2_norm, l0_ffn2_w_in, l0_ffn2_w_out,
           l1_mod_w, l1_mod_b, l1_ffn1_norm, l1_ffn1_w_in, l1_ffn1_w_out, l1_mix_norm, l1_w_in,
           l1_gqa_q_gain, l1_gqa_k_gain, l1_na_q_gain, l1_na_k_gain, l1_na_rpb,
           l1_w_out, l1_ffn2_norm, l1_ffn2_w_in, l1_ffn2_w_out):
    layers = (
        dict(mod_w=l0_mod_w, mod_b=l0_mod_b, ffn1_norm=l0_ffn1_norm, ffn1_w_in=l0_ffn1_w_in,
             ffn1_w_out=l0_ffn1_w_out, mix_norm=l0_mix_norm, w_in=l0_w_in,
             mla_q_norm=l0_mla_q_norm, mla_w_uq=l0_mla_w_uq, mla_kv_norm=l0_mla_kv_norm,
             mla_w_ukv=l0_mla_w_ukv, mla_q_gain=l0_mla_q_gain, mla_k_gain=l0_mla_k_gain,
             conv_glu_b=l0_conv_glu_b, conv_dw_w=l0_conv_dw_w, conv_dw_b=l0_conv_dw_b,
             conv_ln_g=l0_conv_ln_g, conv_ln_b=l0_conv_ln_b, w_out=l0_w_out,
             ffn2_norm=l0_ffn2_norm, ffn2_w_in=l0_ffn2_w_in, ffn2_w_out=l0_ffn2_w_out),
        dict(mod_w=l1_mod_w, mod_b=l1_mod_b, ffn1_norm=l1_ffn1_norm, ffn1_w_in=l1_ffn1_w_in,
             ffn1_w_out=l1_ffn1_w_out, mix_norm=l1_mix_norm, w_in=l1_w_in,
             gqa_q_gain=l1_gqa_q_gain, gqa_k_gain=l1_gqa_k_gain, na_q_gain=l1_na_q_gain,
             na_k_gain=l1_na_k_gain, na_rpb=l1_na_rpb, w_out=l1_w_out,
             ffn2_norm=l1_ffn2_norm, ffn2_w_in=l1_ffn2_w_in, ffn2_w_out=l1_ffn2_w_out),
    )
    batch = x.shape[0]
    xl, xc = x, _flat_ctx(ctx, batch)
    for i, prm in enumerate(layers):
        mods = _layer_mods(c, c_ctx, prm)
        xl, xc = _trunk_layer(xl, xc, mods, prm, even=(i % 2 == 0), ctx_out=(i < len(layers) - 1))
    return xl
```
